```python
import math
import jax, jax.numpy as jnp
from jax import lax
import numpy as np

D_MODEL = 1024
BATCH = 8
SEQ = 2048
DEPTH = 2
DEC_BATCH = 128
DEC_SEQ = 8
PAST_LEN = 16384
PAGE_SIZE = 128

D_MIX = 2 * D_MODEL
HG_WIDTH = D_MIX // 2
HG_KDIM = 128
HG_HEADS = HG_WIDTH // HG_KDIM
HG_VDIM = HG_WIDTH // HG_HEADS
HG_CHUNK = 32
LB_FLOOR = 1e-30
M_DINNER = D_MIX - HG_WIDTH
M_HEADDIM = 64
M_HEADS = M_DINNER // M_HEADDIM
M_DSTATE = 128
M_GROUPS = 2
CONV_W = 4
CONV_DIM = M_DINNER + 2 * M_GROUPS * M_DSTATE
M_CHUNK = 64
DT_MIN = 0.001
DT_MAX = 0.1
D_FF = 2816
EPS = 1e-6
PROJ_SIZES = (HG_HEADS * HG_KDIM, HG_HEADS * HG_KDIM, HG_HEADS * HG_VDIM, HG_HEADS * HG_VDIM, M_DINNER, CONV_DIM, M_HEADS)
D_PROJ = sum(PROJ_SIZES)
PROJ_SPLITS = tuple(int(s) for s in np.cumsum(PROJ_SIZES)[:-1])

kernel_name = "hymba_hgrn2_mamba2_macaron_step"


def _rmsnorm(x, w):
    xf = x.astype(jnp.float32)
    y = xf * lax.rsqrt(jnp.mean(xf * xf, axis=-1, keepdims=True) + EPS)
    return (y * w.astype(jnp.float32)).astype(x.dtype)


def _swiglu(x, wg, wu, wd):
    return (jax.nn.silu(x @ wg) * (x @ wu)) @ wd


def _chunk_len(L, c):
    c = min(c, L)
    return c if L % c == 0 else L


def _gla_chunked(q, k, v, g, s0):
    Bsz, L, H, K = q.shape
    V = v.shape[-1]
    c = _chunk_len(L, HG_CHUNK)
    n = L // c

    def to_chunks(a):
        return a.reshape(Bsz, n, c, H, a.shape[-1]).transpose(1, 0, 3, 2, 4)

    tril = jnp.tril(jnp.ones((c, c), dtype=bool))[:, :, None]

    def step(S, inp):
        qc, kc, vc, gc = inp
        b = jnp.cumsum(gc, axis=2)
        o_inter = jnp.einsum('bhtk,bhkv->bhtv', qc * jnp.exp(b), S)
        diff = b[:, :, :, None, :] - b[:, :, None, :, :]
        decay = jnp.where(tril, jnp.exp(jnp.where(tril, diff, 0.0)), 0.0)
        A = jnp.einsum('bhtk,bhsk,bhtsk->bhts', qc, kc, decay)
        o_intra = jnp.einsum('bhts,bhsv->bhtv', A, vc)
        b_last = b[:, :, -1:, :]
        S_new = jnp.exp(b_last[:, :, 0, :])[..., None] * S + jnp.einsum('bhsk,bhsv->bhkv', kc * jnp.exp(b_last - b), vc)
        return S_new, o_inter + o_intra

    s_fin, o = lax.scan(step, s0, (to_chunks(q), to_chunks(k), to_chunks(v), to_chunks(g)))
    o = o.transpose(1, 0, 3, 2, 4).reshape(Bsz, L, H, V)
    return o, s_fin


def _ssd_chunked(x, dt, a, Bh, Ch, s0):
    Bsz, L, H, P = x.shape
    c = _chunk_len(L, M_CHUNK)
    n = L // c

    def to_chunks(t):
        return t.reshape((Bsz, n, c) + t.shape[2:]).swapaxes(0, 1)

    tril = jnp.tril(jnp.ones((c, c), dtype=bool))[:, :, None]

    def step(S, inp):
        xc, dtc, ac, bc, cc = inp
        cum = jnp.cumsum(ac, axis=1)
        seg = cum[:, :, None, :] - cum[:, None, :, :]
        Lm = jnp.where(tril, jnp.exp(jnp.where(tril, seg, 0.0)), 0.0)
        scores = jnp.einsum('bthn,bshn->btsh', cc, bc) * Lm
        y_intra = jnp.einsum('btsh,bshp->bthp', scores, xc * dtc[..., None])
        y_inter = jnp.einsum('bthn,bhpn->bthp', cc * jnp.exp(cum)[..., None], S)
        last = cum[:, -1:, :]
        S_new = jnp.exp(last[:, 0, :])[:, :, None, None] * S + jnp.einsum('bshn,bshp->bhpn', bc * (dtc * jnp.exp(last - cum))[..., None], xc)
        return S_new, y_intra + y_inter

    s_fin, y = lax.scan(step, s0, (to_chunks(x), to_chunks(dt), to_chunks(dt * a), to_chunks(Bh), to_chunks(Ch)))
    y = y.swapaxes(0, 1).reshape(Bsz, L, H, P)
    return y, s_fin


def _causal_conv(u, buf, w, b):
    L = u.shape[1]
    cat = jnp.concatenate([buf.astype(u.dtype), u], axis=1)
    out = b
    for j in range(CONV_W):
        out = out + cat[:, j:j + L] * w[j]
    return out, cat[:, L:]


def _mixer(xn, lb, s_hg, s_ssm, s_conv, w_in, conv_w, conv_b, dt_bias, a_log, d_skip, hg_norm_w, ssm_norm_w, w_out):
    f32 = jnp.float32
    Bsz, L, _ = xn.shape
    u = xn @ w_in
    q, fr, iv, go, z, xbc, dt = jnp.split(u, PROJ_SPLITS, axis=-1)
    q = q.reshape(Bsz, L, HG_HEADS, HG_KDIM).astype(f32) * (HG_KDIM ** -0.5)
    fr = fr.reshape(Bsz, L, HG_HEADS, HG_KDIM).astype(f32)
    lb = lb.reshape(HG_HEADS, HG_KDIM).astype(f32)
    g_log = jnp.logaddexp(jnp.log(jnp.maximum(lb, LB_FLOOR)), jnp.log1p(-lb) + jax.nn.log_sigmoid(fr))
    k = (1.0 - lb) * jax.nn.sigmoid(-fr)
    v = iv.reshape(Bsz, L, HG_HEADS, HG_VDIM).astype(f32)
    o_hg, s_hg_new = _gla_chunked(q, k, v, g_log, s_hg.astype(f32))
    o_hg = _rmsnorm(o_hg, hg_norm_w.reshape(HG_HEADS, HG_VDIM)) * jax.nn.silu(go.reshape(Bsz, L, HG_HEADS, HG_VDIM).astype(f32))
    o_hg = o_hg.reshape(Bsz, L, HG_WIDTH)
    xbc_c, s_conv_new = _causal_conv(xbc, s_conv, conv_w, conv_b)
    xbc_c = jax.nn.silu(xbc_c).astype(f32)
    xs, Bm, Cm = jnp.split(xbc_c, (M_DINNER, M_DINNER + M_GROUPS * M_DSTATE), axis=-1)
    xs = xs.reshape(Bsz, L, M_HEADS, M_HEADDIM)
    rep = M_HEADS // M_GROUPS
    Bh = jnp.repeat(Bm.reshape(Bsz, L, M_GROUPS, M_DSTATE), rep, axis=2)
    Ch = jnp.repeat(Cm.reshape(Bsz, L, M_GROUPS, M_DSTATE), rep, axis=2)
    dt = jax.nn.softplus(dt.astype(f32) + dt_bias.astype(f32))
    A = -jnp.exp(a_log.astype(f32))
    y, s_ssm_new = _ssd_chunked(xs, dt, A, Bh, Ch, s_ssm.astype(f32))
    y = y + d_skip.astype(f32)[:, None] * xs
    y = y.reshape(Bsz, L, M_DINNER) * jax.nn.silu(z.astype(f32))
    y = _rmsnorm(y.reshape(Bsz, L, M_GROUPS, M_DINNER // M_GROUPS), ssm_norm_w.reshape(M_GROUPS, M_DINNER // M_GROUPS))
    y = y.reshape(Bsz, L, M_DINNER)
    o = jnp.concatenate([o_hg, y], axis=-1).astype(xn.dtype) @ w_out
    return o, s_hg_new.astype(s_hg.dtype), s_ssm_new.astype(s_ssm.dtype), s_conv_new.astype(s_conv.dtype)


def _layer(x, s_hg, s_ssm, s_conv, lb, gains, w_in, conv_w, conv_b, dt_bias, a_log, d_skip, hg_norm_w, ssm_norm_w, w_out, f1g, f1u, f1d, f2g, f2u, f2d):
    h = x + 0.5 * _rmsnorm(_swiglu(_rmsnorm(x, gains[0]), f1g, f1u, f1d), gains[1])
    m, s_hg, s_ssm, s_conv = _mixer(_rmsnorm(h, gains[2]), lb, s_hg, s_ssm, s_conv, w_in, conv_w, conv_b, dt_bias, a_log, d_skip, hg_norm_w, ssm_norm_w, w_out)
    h = h + _rmsnorm(m, gains[3])
    h = h + 0.5 * _rmsnorm(_swiglu(_rmsnorm(h, gains[4]), f2g, f2u, f2d), gains[5])
    return h, s_hg, s_ssm, s_conv


def setup_inputs(seed: int = 0) -> dict:
    key = jax.random.key(seed)
    ks = jax.random.split(key, 24)
    f32 = jnp.float32

    def nrm(k, shape, fan):
        return jax.random.normal(k, shape, f32) * (fan ** -0.5)

    def gauss(k, shape, s=1.0):
        return jax.random.normal(k, shape, f32) * s

    dt = jnp.exp(jax.random.uniform(ks[9], (DEPTH, M_HEADS), f32) * (math.log(DT_MAX) - math.log(DT_MIN)) + math.log(DT_MIN))
    return {
        "x_prompt": gauss(ks[0], (BATCH, SEQ, D_MODEL)),
        "x_sample": gauss(ks[1], (DEC_BATCH, DEC_SEQ, D_MODEL)),
        "state_hgrn": gauss(ks[2], (DEPTH, DEC_BATCH, HG_HEADS, HG_KDIM, HG_VDIM), 0.5),
        "state_ssm": gauss(ks[3], (DEPTH, DEC_BATCH, M_HEADS, M_HEADDIM, M_DSTATE), 0.5),
        "state_conv": gauss(ks[4], (DEPTH, DEC_BATCH, CONV_W - 1, CONV_DIM)),
        "w_in": nrm(ks[5], (DEPTH, D_MODEL, D_PROJ), D_MODEL),
        "hg_lb_logits": gauss(ks[6], (DEPTH, HG_HEADS * HG_KDIM)),
        "conv_w": nrm(ks[7], (DEPTH, CONV_W, CONV_DIM), CONV_W),
        "conv_b": gauss(ks[8], (DEPTH, CONV_DIM), 0.02),
        "dt_bias": dt + jnp.log(-jnp.expm1(-dt)),
        "a_log": jnp.log(jax.random.uniform(ks[10], (DEPTH, M_HEADS), f32, minval=1.0, maxval=16.0)),
        "d_skip": 1.0 + gauss(ks[11], (DEPTH, M_HEADS), 0.1),
        "hg_norm_w": 1.0 + gauss(ks[12], (DEPTH, HG_WIDTH), 0.05),
        "ssm_norm_w": 1.0 + gauss(ks[13], (DEPTH, M_DINNER), 0.05),
        "w_out": nrm(ks[14], (DEPTH, D_MIX, D_MODEL), D_MIX),
        "ffn1_w_gate": nrm(ks[15], (DEPTH, D_MODEL, D_FF), D_MODEL),
        "ffn1_w_up": nrm(ks[16], (DEPTH, D_MODEL, D_FF), D_MODEL),
        "ffn1_w_down": nrm(ks[17], (DEPTH, D_FF, D_MODEL), D_FF),
        "ffn2_w_gate": nrm(ks[18], (DEPTH, D_MODEL, D_FF), D_MODEL),
        "ffn2_w_up": nrm(ks[19], (DEPTH, D_MODEL, D_FF), D_MODEL),
        "ffn2_w_down": nrm(ks[20], (DEPTH, D_FF, D_MODEL), D_FF),
        "norm_gain": 1.0 + gauss(ks[21], (DEPTH, 6, D_MODEL), 0.05),
    }


def reference(x_prompt, x_sample, state_hgrn, state_ssm, state_conv, w_in, hg_lb_logits, conv_w, conv_b, dt_bias, a_log, d_skip, hg_norm_w, ssm_norm_w, w_out, ffn1_w_gate, ffn1_w_up, ffn1_w_down, ffn2_w_gate, ffn2_w_up, ffn2_w_down, norm_gain):
    p = jax.nn.softmax(hg_lb_logits.astype(jnp.float32), axis=0)
    lb_all = jnp.cumsum(p, axis=0) - p[0]

    def trunk(x, s_hg, s_ssm, s_conv):
        hg_out, ssm_out, conv_out = [], [], []
        for l in range(DEPTH):
            x, a, b, c = _layer(x, s_hg[l], s_ssm[l], s_conv[l], lb_all[l], norm_gain[l], w_in[l], conv_w[l], conv_b[l], dt_bias[l], a_log[l], d_skip[l], hg_norm_w[l], ssm_norm_w[l], w_out[l], ffn1_w_gate[l], ffn1_w_up[l], ffn1_w_down[l], ffn2_w_gate[l], ffn2_w_up[l], ffn2_w_down[l])
            hg_out.append(a)
            ssm_out.append(b)
            conv_out.append(c)
        return x, jnp.stack(hg_out), jnp.stack(ssm_out), jnp.stack(conv_out)

    bp = x_prompt.shape[0]
    dtp = x_prompt.dtype
    y_prompt, hg_p, ssm_p, conv_p = trunk(
        x_prompt,
        jnp.zeros((DEPTH, bp, HG_HEADS, HG_KDIM, HG_VDIM), dtp),
        jnp.zeros((DEPTH, bp, M_HEADS, M_HEADDIM, M_DSTATE), dtp),
        jnp.zeros((DEPTH, bp, CONV_W - 1, CONV_DIM), dtp))
    y_sample, hg_s, ssm_s, conv_s = trunk(x_sample, state_hgrn, state_ssm, state_conv)
    return (y_prompt, y_sample, hg_p, ssm_p, conv_p, hg_s, ssm_s, conv_s)
```

```python
import functools

import jax
import jax.numpy as jnp
from jax import lax
from jax.experimental import pallas as pl
from jax.experimental.pallas import tpu as pltpu

F32 = jnp.float32
BF16 = jnp.bfloat16
HIGHEST = lax.Precision.HIGHEST

D_MODEL = 1024
DEPTH = 2
HG_HEADS = 8
HG_KDIM = 128
HG_VDIM = 128
HG_WIDTH = HG_HEADS * HG_KDIM
LB_FLOOR = 1e-30
M_DINNER = 1024
M_HEADDIM = 64
M_HEADS = 16
M_DSTATE = 128
M_GROUPS = 2
M_GROUP_WIDTH = M_DINNER // M_GROUPS
CONV_W = 4
CONV_DIM = M_DINNER + 2 * M_GROUPS * M_DSTATE
D_FF = 2816
EPS = 1e-6

LANES = 128
SUBLANES = 8
DT_PAD = LANES
D_PROJ_PAD = 4 * HG_WIDTH + M_DINNER + CONV_DIM + DT_PAD
NEG_BIG = -1e30
VMEM_LIMIT = 56 * 1024 * 1024

_NT = (((1,), (1,)), ((), ()))
_TN = (((0,), (0,)), ((), ()))


def _rms(x, w):
    return x * lax.rsqrt(jnp.mean(x * x, axis=-1, keepdims=True) + EPS) * w


def _silu(x):
    return x * jax.nn.sigmoid(x)


def _mm(a, b):
    return jnp.dot(a.astype(BF16), b.astype(BF16), preferred_element_type=F32)


def _mm_nt(a, b):
    return lax.dot_general(a.astype(BF16), b.astype(BF16), _NT, preferred_element_type=F32)


def _mm_tn(a, b):
    return lax.dot_general(a.astype(BF16), b.astype(BF16), _TN, preferred_element_type=F32)


def _resident(shape):
    nd = len(shape)
    return pl.BlockSpec(shape, lambda *_: (0,) * nd, pipeline_mode=pl.Buffered(1))


def _token_tile(n_tokens, largest=512):
    for tm in (t for t in (512, 256, 128, 64, 32, 16, 8) if t <= largest):
        if n_tokens % tm == 0:
            return tm
    raise ValueError(f"token count {n_tokens} is not a multiple of 8")


FF_CHUNK = 256


def _ffn_body(x_ref, g_ref, wg_ref, wu_ref, wd_ref, o_ref):
    x = x_ref[...]
    xn = _rms(x, g_ref[0:1, :]).astype(BF16)
    acc = None
    for j in range(D_FF // FF_CHUNK):
        sl = slice(j * FF_CHUNK, (j + 1) * FF_CHUNK)
        gate = jnp.dot(xn, wg_ref[:, sl], preferred_element_type=F32)
        up = jnp.dot(xn, wu_ref[:, sl], preferred_element_type=F32)
        act = (_silu(gate) * up).astype(BF16)
        part = jnp.dot(act, wd_ref[sl, :], preferred_element_type=F32)
        acc = part if acc is None else acc + part
    o_ref[...] = x + 0.5 * _rms(acc, g_ref[1:2, :])


def _ffn(x, gains2, wg, wu, wd):
    n = x.shape[0]
    tm = _token_tile(n)
    return pl.pallas_call(
        _ffn_body,
        out_shape=jax.ShapeDtypeStruct((n, D_MODEL), F32),
        grid=(n // tm,),
        in_specs=[
            pl.BlockSpec((tm, D_MODEL), lambda i: (i, 0)),
            _resident((2, D_MODEL)),
            _resident((D_MODEL, D_FF)),
            _resident((D_MODEL, D_FF)),
            _resident((D_FF, D_MODEL)),
        ],
        out_specs=pl.BlockSpec((tm, D_MODEL), lambda i: (i, 0)),
        compiler_params=pltpu.CompilerParams(dimension_semantics=("arbitrary",), vmem_limit_bytes=VMEM_LIMIT),
        name="ffn",
    )(x, gains2, wg, wu, wd)


_C_Q = 0
_C_F = HG_WIDTH
_C_I = 2 * HG_WIDTH
_C_G = 3 * HG_WIDTH
_C_Z = 4 * HG_WIDTH
_C_X = _C_Z + M_DINNER
_C_DT = _C_X + CONV_DIM


def _inproj_body(layer, x_ref, g_ref, lbl_ref, dtb_ref, w_ref,
                 q_ref, k_ref, gl_ref, v_ref, go_ref, z_ref, xbc_ref, dt_ref):
    xn = _rms(x_ref[...], g_ref[...]).astype(BF16)

    def proj(c0, width):
        return jnp.dot(xn, w_ref[:, c0:c0 + width], preferred_element_type=F32)

    lg = lbl_ref[...]
    e = jnp.exp(lg - jnp.max(lg, axis=0, keepdims=True))
    p = e / jnp.sum(e, axis=0, keepdims=True)
    lb = jnp.sum(p[0:layer + 1, :], axis=0, keepdims=True) - p[0:1, :]
    log_lb = jnp.log(jnp.maximum(lb, LB_FLOOR))
    log_1mlb = jnp.log1p(-lb)

    q = proj(_C_Q, HG_WIDTH) * (HG_KDIM ** -0.5)
    fr = proj(_C_F, HG_WIDTH)
    glog = jnp.logaddexp(log_lb, log_1mlb + jax.nn.log_sigmoid(fr))
    kk = (1.0 - lb) * jax.nn.sigmoid(-fr)
    v = proj(_C_I, HG_WIDTH)
    go = _silu(proj(_C_G, HG_WIDTH))
    for h in range(HG_HEADS):
        sl = slice(h * HG_KDIM, (h + 1) * HG_KDIM)
        q_ref[h] = q[:, sl]
        k_ref[h] = kk[:, sl]
        gl_ref[h] = glog[:, sl]
        v_ref[h] = v[:, sl]
        go_ref[h] = go[:, sl]
    z_ref[...] = _silu(proj(_C_Z, M_DINNER))
    xbc_ref[...] = proj(_C_X, CONV_DIM)
    dt_ref[...] = jax.nn.softplus(proj(_C_DT, DT_PAD) + dtb_ref[...])


def _inproj(h, gain, lb_logits, dt_bias_pad, w, layer):
    n = h.shape[0]
    tm = _token_tile(n, largest=256)
    head_major = jax.ShapeDtypeStruct((HG_HEADS, n, HG_KDIM), F32)
    head_spec = pl.BlockSpec((HG_HEADS, tm, HG_KDIM), lambda i: (0, i, 0))

    def tok(width):
        return pl.BlockSpec((tm, width), lambda i: (i, 0))

    return pl.pallas_call(
        functools.partial(_inproj_body, layer),
        out_shape=(head_major,) * 5 + (
            jax.ShapeDtypeStruct((n, M_DINNER), F32),
            jax.ShapeDtypeStruct((n, CONV_DIM), F32),
            jax.ShapeDtypeStruct((n, DT_PAD), F32),
        ),
        grid=(n // tm,),
        in_specs=[
            tok(D_MODEL),
            _resident((1, D_MODEL)),
            _resident((DEPTH, HG_WIDTH)),
            _resident((1, DT_PAD)),
            _resident((D_MODEL, D_PROJ_PAD)),
        ],
        out_specs=(head_spec,) * 5 + (tok(M_DINNER), tok(CONV_DIM), tok(DT_PAD)),
        compiler_params=pltpu.CompilerParams(dimension_semantics=("arbitrary",), vmem_limit_bytes=VMEM_LIMIT),
        name="inproj",
    )(h, gain, lb_logits, dt_bias_pad, w)


DIAG = SUBLANES


def _expand_heads(vals, group):
    rows = vals.shape[0]
    lane = lax.broadcasted_iota(jnp.int32, (rows, LANES), 1)
    tiles = []
    for j in range(M_GROUP_WIDTH // LANES):
        h0 = group * (M_HEADS // M_GROUPS) + 2 * j
        lo = jnp.broadcast_to(vals[:, h0:h0 + 1], (rows, LANES))
        hi = jnp.broadcast_to(vals[:, h0 + 1:h0 + 2], (rows, LANES))
        tiles.append(jnp.where(lane < M_HEADDIM, lo, hi))
    return jnp.concatenate(tiles, axis=1)


def _mixer_body(nseq, seq_chunk,
                q_ref, k_ref, gl_ref, v_ref, go_ref, z_ref, xbc_ref, dt_ref,
                hg0_ref, ssm0_ref, conv0_ref,
                convw_ref, convb_ref, alog_ref, dskip_ref, hgw_ref, ssmw_ref,
                ohg_ref, yssm_ref, hg_out_ref, ssm_out_ref, conv_out_ref,
                st_hg, st_ssm, cat_ref):
    chunk = nseq * seq_chunk
    step = pl.program_id(1)
    last_step = pl.num_programs(1) - 1
    heads_per_group = M_HEADS // M_GROUPS

    @pl.when(step == 0)
    def _load_states():
        for i in range(nseq):
            for h in range(HG_HEADS):
                st_hg[i, h] = hg0_ref[i, h].T
            for g in range(M_GROUPS):
                s0 = ssm0_ref[i, g * heads_per_group:(g + 1) * heads_per_group]
                st_ssm[i, g] = s0.reshape(M_GROUP_WIDTH, M_DSTATE).T
            cat_ref[i, 0:SUBLANES, :] = conv0_ref[i]

    row = lax.broadcasted_iota(jnp.int32, (chunk, chunk), 0)
    col = lax.broadcasted_iota(jnp.int32, (chunk, chunk), 1)
    causal = (col <= row) & ((row // seq_chunk) == (col // seq_chunk))
    causal_f = causal.astype(F32)
    row1 = lax.broadcasted_iota(jnp.int32, (chunk, 1), 0)
    lane_c = lax.broadcasted_iota(jnp.int32, (DIAG, chunk), 1)
    sub_k = lax.broadcasted_iota(jnp.int32, (DIAG, HG_KDIM), 0)
    seq_rows = [slice(i * seq_chunk, (i + 1) * seq_chunk) for i in range(nseq)]

    for h in range(HG_HEADS):
        q = q_ref[h]
        k = k_ref[h]
        v = v_ref[h]
        b = jnp.dot(causal_f, gl_ref[h], precision=HIGHEST, preferred_element_type=F32)
        qe = q * jnp.exp(b)
        o = jnp.concatenate([_mm_nt(qe[r], st_hg[i, h]) for i, r in enumerate(seq_rows)], axis=0)

        scores = jnp.zeros((chunk, chunk), F32)
        m = DIAG
        while 2 * m <= seq_chunk:
            blk = 2 * m
            bref = jnp.concatenate(
                [jnp.broadcast_to(b[j * blk + m - 1:j * blk + m, :], (blk, HG_KDIM)) for j in range(chunk // blk)],
                axis=0)
            upper = (row1 % blk) >= m
            e = jnp.exp(jnp.where(upper, b - bref, bref - b))
            qt = jnp.where(upper, q * e, 0.0)
            kt = jnp.where(upper, 0.0, k * e)
            scores = scores + jnp.where((row // blk) == (col // blk), _mm_nt(qt, kt), 0.0)
            m = blk

        tiles = []
        for sb in range(chunk // DIAG):
            r0 = sb * DIAG
            qs = q[r0:r0 + DIAG]
            bs = b[r0:r0 + DIAG]
            tile = jnp.zeros((DIAG, chunk), F32)
            for j in range(DIAG):
                s = r0 + j
                e = jnp.exp(jnp.where(sub_k >= j, bs - b[s:s + 1, :], NEG_BIG))
                colv = jnp.sum(qs * k[s:s + 1, :] * e, axis=-1, keepdims=True)
                tile = jnp.where(lane_c == s, colv, tile)
            tiles.append(tile)
        scores = scores + jnp.concatenate(tiles, axis=0)
        o = o + _mm(scores, v)

        for i, r in enumerate(seq_rows):
            b_last = b[r.stop - 1:r.stop, :]
            kd = k[r] * jnp.exp(b_last - b[r])
            st_hg[i, h] = jnp.exp(b_last) * st_hg[i, h] + _mm_tn(v[r], kd)

        on = _rms(o, hgw_ref[h]) * go_ref[h]
        ohg_ref[:, h * HG_VDIM:(h + 1) * HG_VDIM] = on.astype(BF16)

    acts = []
    for i, r in enumerate(seq_rows):
        cat_ref[i, SUBLANES:SUBLANES + seq_chunk, :] = xbc_ref[r, :]
        conv = convb_ref[...]
        for j in range(CONV_W):
            off = SUBLANES - (CONV_W - 1) + j
            conv = conv + cat_ref[i, off:off + seq_chunk, :] * convw_ref[j:j + 1, :]
        acts.append(_silu(conv))
        cat_ref[i, 0:SUBLANES, :] = cat_ref[i, seq_chunk:seq_chunk + SUBLANES, :]
    act = acts[0] if nseq == 1 else jnp.concatenate(acts, axis=0)
    xs = act[:, 0:M_DINNER]

    lane_t = lax.broadcasted_iota(jnp.int32, (chunk, DT_PAD), 1)
    dt = jnp.where(lane_t < M_HEADS, dt_ref[...], 0.0)
    da = dt * (-jnp.exp(alog_ref[...]))
    cum = jnp.dot(causal_f, da, precision=HIGHEST, preferred_element_type=F32)
    cum_t = cum.T
    dt_t = dt.T
    ecum = jnp.exp(cum)
    lane_x = lax.broadcasted_iota(jnp.int32, (chunk, LANES), 1)

    y_groups = []
    for g in range(M_GROUPS):
        b_g = act[:, M_DINNER + g * M_DSTATE:M_DINNER + (g + 1) * M_DSTATE]
        c_g = act[:, M_DINNER + (M_GROUPS + g) * M_DSTATE:M_DINNER + (M_GROUPS + g + 1) * M_DSTATE]
        x_g = xs[:, g * M_GROUP_WIDTH:(g + 1) * M_GROUP_WIDTH]
        cb = _mm_nt(c_g, b_g)
        y_tiles = []
        for j in range(M_GROUP_WIDTH // LANES):
            x_pair = x_g[:, j * LANES:(j + 1) * LANES]
            y_pair = None
            for half in range(2):
                hd = g * heads_per_group + 2 * j + half
                diff = jnp.broadcast_to(cum[:, hd:hd + 1], (chunk, chunk)) - jnp.broadcast_to(cum_t[hd:hd + 1, :], (chunk, chunk))
                decay = jnp.exp(jnp.where(causal, diff, NEG_BIG))
                mh = cb * decay * jnp.broadcast_to(dt_t[hd:hd + 1, :], (chunk, chunk))
                keep = (lane_x < M_HEADDIM) if half == 0 else (lane_x >= M_HEADDIM)
                part = _mm(mh, jnp.where(keep, x_pair, 0.0))
                y_pair = part if y_pair is None else y_pair + part
            y_tiles.append(y_pair)
        y_intra = jnp.concatenate(y_tiles, axis=1)
        y_inter = jnp.concatenate([_mm(c_g[r], st_ssm[i, g]) for i, r in enumerate(seq_rows)], axis=0)
        y_g = y_intra + y_inter * _expand_heads(ecum, g)
        y_g = y_g + dskip_ref[:, g * M_GROUP_WIDTH:(g + 1) * M_GROUP_WIDTH] * x_g
        y_g = y_g * z_ref[:, g * M_GROUP_WIDTH:(g + 1) * M_GROUP_WIDTH]
        y_groups.append(_rms(y_g, ssmw_ref[:, g * M_GROUP_WIDTH:(g + 1) * M_GROUP_WIDTH]))

        for i, r in enumerate(seq_rows):
            cum_last = cum[r.stop - 1:r.stop, :]
            wgt = dt[r] * jnp.exp(cum_last - cum[r])
            xw = x_g[r] * _expand_heads(wgt, g)
            st_ssm[i, g] = _expand_heads(jnp.exp(cum_last), g) * st_ssm[i, g] + _mm_tn(b_g[r], xw)
    yssm_ref[...] = jnp.concatenate(y_groups, axis=1).astype(BF16)

    @pl.when(step == last_step)
    def _store_states():
        for i in range(nseq):
            for h in range(HG_HEADS):
                hg_out_ref[i, h] = st_hg[i, h].T
            for g in range(M_GROUPS):
                ssm_out_ref[i, g * heads_per_group:(g + 1) * heads_per_group] = (
                    st_ssm[i, g].T.reshape(heads_per_group, M_HEADDIM, M_DSTATE))
            conv_out_ref[i] = cat_ref[i, 0:SUBLANES, :]


def _mixer(heads5, zs, xbc, dt, hg0, ssm0, conv0, convw, convb, alog, dskip, hgw, ssmw, *, batch, seq_len):
    if seq_len >= LANES:
        nseq, seq_chunk = 1, LANES
    else:
        nseq, seq_chunk = 4, seq_len
    assert seq_len % seq_chunk == 0 and batch % nseq == 0 and seq_chunk % DIAG == 0
    chunk = nseq * seq_chunk
    steps = seq_len // seq_chunk
    n = batch * seq_len

    def tok_idx(b, s):
        return b * steps + s

    head_spec = pl.BlockSpec((HG_HEADS, chunk, HG_KDIM), lambda b, s: (0, tok_idx(b, s), 0))

    def tok(width):
        return pl.BlockSpec((chunk, width), lambda b, s: (tok_idx(b, s), 0))

    hg_spec = pl.BlockSpec((nseq, HG_HEADS, HG_KDIM, HG_VDIM), lambda b, s: (b, 0, 0, 0))
    ssm_spec = pl.BlockSpec((nseq, M_HEADS, M_HEADDIM, M_DSTATE), lambda b, s: (b, 0, 0, 0))
    conv_spec = pl.BlockSpec((nseq, SUBLANES, CONV_DIM), lambda b, s: (b, 0, 0))

    return pl.pallas_call(
        functools.partial(_mixer_body, nseq, seq_chunk),
        out_shape=(
            jax.ShapeDtypeStruct((n, HG_WIDTH), BF16),
            jax.ShapeDtypeStruct((n, M_DINNER), BF16),
            jax.ShapeDtypeStruct(hg0.shape, F32),
            jax.ShapeDtypeStruct(ssm0.shape, F32),
            jax.ShapeDtypeStruct(conv0.shape, F32),
        ),
        grid=(batch // nseq, steps),
        in_specs=[head_spec] * 5 + [tok(M_DINNER), tok(CONV_DIM), tok(DT_PAD), hg_spec, ssm_spec, conv_spec,
                                    _resident((CONV_W, CONV_DIM)), _resident((1, CONV_DIM)), _resident((1, DT_PAD)),
                                    _resident((1, M_DINNER)), _resident((HG_HEADS, 1, HG_VDIM)), _resident((1, M_DINNER))],
        out_specs=(tok(HG_WIDTH), tok(M_DINNER), hg_spec, ssm_spec, conv_spec),
        scratch_shapes=[
            pltpu.VMEM((nseq, HG_HEADS, HG_VDIM, HG_KDIM), F32),
            pltpu.VMEM((nseq, M_GROUPS, M_DSTATE, M_GROUP_WIDTH), F32),
            pltpu.VMEM((nseq, SUBLANES + seq_chunk, CONV_DIM), F32),
        ],
        compiler_params=pltpu.CompilerParams(dimension_semantics=("arbitrary", "arbitrary"),
                                             vmem_limit_bytes=VMEM_LIMIT),
        name="mixer",
    )(*heads5, zs, xbc, dt, hg0, ssm0, conv0, convw, convb, alog, dskip, hgw, ssmw)


def _outproj_body(h_ref, ohg_ref, yssm_ref, g_ref, w_ref, o_ref):
    m = jnp.dot(ohg_ref[...], w_ref[0:HG_WIDTH, :], preferred_element_type=F32)
    m = m + jnp.dot(yssm_ref[...], w_ref[HG_WIDTH:HG_WIDTH + M_DINNER, :], preferred_element_type=F32)
    o_ref[...] = h_ref[...] + _rms(m, g_ref[...])


def _outproj(h, ohg, yssm, gain, w):
    n = h.shape[0]
    tm = _token_tile(n)

    def tok(width):
        return pl.BlockSpec((tm, width), lambda i: (i, 0))

    return pl.pallas_call(
        _outproj_body,
        out_shape=jax.ShapeDtypeStruct((n, D_MODEL), F32),
        grid=(n // tm,),
        in_specs=[tok(D_MODEL), tok(HG_WIDTH), tok(M_DINNER), _resident((1, D_MODEL)),
                  _resident((HG_WIDTH + M_DINNER, D_MODEL))],
        out_specs=tok(D_MODEL),
        compiler_params=pltpu.CompilerParams(dimension_semantics=("arbitrary",), vmem_limit_bytes=VMEM_LIMIT),
        name="outproj",
    )(h, ohg, yssm, gain, w)


def _prep_layer(l, w_in, hg_lb_logits, conv_w, conv_b, dt_bias, a_log, d_skip, hg_norm_w, ssm_norm_w, w_out,
                f1g, f1u, f1d, f2g, f2u, f2d, norm_gain):
    pad = DT_PAD - M_HEADS
    return dict(
        layer=l,
        w_in=jnp.pad(w_in[l], ((0, 0), (0, pad))).astype(BF16),
        lb_logits=hg_lb_logits,
        conv_w=conv_w[l],
        conv_b=conv_b[l][None, :],
        dt_bias=jnp.pad(dt_bias[l], (0, pad))[None, :],
        a_log=jnp.pad(a_log[l], (0, pad))[None, :],
        d_skip=jnp.repeat(d_skip[l], M_HEADDIM)[None, :],
        hg_norm_w=hg_norm_w[l].reshape(HG_HEADS, 1, HG_VDIM),
        ssm_norm_w=ssm_norm_w[l][None, :],
        w_out=w_out[l].astype(BF16),
        ffn1=(f1g[l].astype(BF16), f1u[l].astype(BF16), f1d[l].astype(BF16)),
        ffn2=(f2g[l].astype(BF16), f2u[l].astype(BF16), f2d[l].astype(BF16)),
        gains=norm_gain[l],
    )


def _layer(x, hg0, ssm0, conv0, p, *, batch, seq_len):
    gains = p["gains"]
    h = _ffn(x, gains[0:2], *p["ffn1"])
    *heads5, zs, xbc, dt = _inproj(h, gains[2:3], p["lb_logits"], p["dt_bias"], p["w_in"], p["layer"])
    ohg, yssm, hg1, ssm1, conv1 = _mixer(
        heads5, zs, xbc, dt, hg0, ssm0, conv0, p["conv_w"], p["conv_b"], p["a_log"], p["d_skip"],
        p["hg_norm_w"], p["ssm_norm_w"], batch=batch, seq_len=seq_len)
    h = _outproj(h, ohg, yssm, gains[3:4], p["w_out"])
    h = _ffn(h, gains[4:6], *p["ffn2"])
    return h, hg1, ssm1, conv1


def _trunk(x, s_hg, s_ssm, s_conv, layers):
    batch, seq_len, _ = x.shape
    h = x.reshape(batch * seq_len, D_MODEL)
    conv_pad = jnp.pad(s_conv, ((0, 0), (0, 0), (SUBLANES - (CONV_W - 1), 0), (0, 0)))
    hg_out, ssm_out, conv_out = [], [], []
    for l, p in enumerate(layers):
        h, a, b, c = _layer(h, s_hg[l], s_ssm[l], conv_pad[l], p, batch=batch, seq_len=seq_len)
        hg_out.append(a)
        ssm_out.append(b)
        conv_out.append(c[:, SUBLANES - (CONV_W - 1):, :])
    return h.reshape(batch, seq_len, D_MODEL), jnp.stack(hg_out), jnp.stack(ssm_out), jnp.stack(conv_out)


def kernel(x_prompt, x_sample, state_hgrn, state_ssm, state_conv, w_in, hg_lb_logits, conv_w, conv_b, dt_bias, a_log, d_skip, hg_norm_w, ssm_norm_w, w_out, ffn1_w_gate, ffn1_w_up, ffn1_w_down, ffn2_w_gate, ffn2_w_up, ffn2_w_down, norm_gain):
    layers = [
        _prep_layer(l, w_in, hg_lb_logits, conv_w, conv_b, dt_bias, a_log, d_skip, hg_norm_w, ssm_norm_w, w_out,
                    ffn1_w_gate, ffn1_w_up, ffn1_w_down, ffn2_w_gate, ffn2_w_up, ffn2_w_down, norm_gain)
        for l in range(DEPTH)
    ]
    bp = x_prompt.shape[0]
    y_prompt, hg_p, ssm_p, conv_p = _trunk(
        x_prompt,
        jnp.zeros((DEPTH, bp, HG_HEADS, HG_KDIM, HG_VDIM), F32),
        jnp.zeros((DEPTH, bp, M_HEADS, M_HEADDIM, M_DSTATE), F32),
        jnp.zeros((DEPTH, bp, CONV_W - 1, CONV_DIM), F32),
        layers)
    y_sample, hg_s, ssm_s, conv_s = _trunk(x_sample, state_hgrn, state_ssm, state_conv, layers)
    return (y_prompt, y_sample, hg_p, ssm_p, conv_p, hg_s, ssm_s, conv_s)
```

```python
import functools

import jax
import jax.numpy as jnp
from jax import lax
from jax.experimental import pallas as pl
from jax.experimental.pallas import tpu as pltpu

F32 = jnp.float32
BF16 = jnp.bfloat16
HIGHEST = lax.Precision.HIGHEST

D_MODEL = 1024
DEPTH = 2
HG_HEADS = 8
HG_KDIM = 128
HG_VDIM = 128
HG_WIDTH = HG_HEADS * HG_KDIM
LB_FLOOR = 1e-30
M_DINNER = 1024
M_HEADDIM = 64
M_HEADS = 16
M_DSTATE = 128
M_GROUPS = 2
M_GROUP_WIDTH = M_DINNER // M_GROUPS
CONV_W = 4
CONV_DIM = M_DINNER + 2 * M_GROUPS * M_DSTATE
D_FF = 2816
EPS = 1e-6

LANES = 128
SUBLANES = 8
DT_PAD = LANES
D_PROJ_MAIN = 4 * HG_WIDTH + M_DINNER + CONV_DIM
INPROJ_TILE = 256
SAMPLE_SEQS_PER_STEP = 4
NEG_BIG = -1e30
VMEM_LIMIT = 56 * 1024 * 1024

_NT = (((1,), (1,)), ((), ()))
_TN = (((0,), (0,)), ((), ()))


def _rms(x, w):
    return x * lax.rsqrt(jnp.mean(x * x, axis=-1, keepdims=True) + EPS) * w


def _silu(x):
    return x * jax.nn.sigmoid(x)


def _mm(a, b):
    return jnp.dot(a.astype(BF16), b.astype(BF16), preferred_element_type=F32)


def _mm_nt(a, b):
    return lax.dot_general(a.astype(BF16), b.astype(BF16), _NT, preferred_element_type=F32)


def _mm_tn(a, b):
    return lax.dot_general(a.astype(BF16), b.astype(BF16), _TN, preferred_element_type=F32)


def _resident(shape):
    nd = len(shape)
    return pl.BlockSpec(shape, lambda *_: (0,) * nd, pipeline_mode=pl.Buffered(1))


def _token_tile(n_tokens, largest=512):
    for tm in (t for t in (512, 256, 128, 64, 32, 16, 8) if t <= largest):
        if n_tokens % tm == 0:
            return tm
    raise ValueError(f"token count {n_tokens} is not a multiple of 8")


FF_CHUNK = 256


def _ffn_body(x_ref, g_ref, wg_ref, wu_ref, wd_ref, o_ref):
    x = x_ref[...]
    xn = _rms(x, g_ref[0:1, :]).astype(BF16)
    acc = None
    for j in range(D_FF // FF_CHUNK):
        sl = slice(j * FF_CHUNK, (j + 1) * FF_CHUNK)
        gate = jnp.dot(xn, wg_ref[:, sl], preferred_element_type=F32)
        up = jnp.dot(xn, wu_ref[:, sl], preferred_element_type=F32)
        act = (_silu(gate) * up).astype(BF16)
        part = jnp.dot(act, wd_ref[sl, :], preferred_element_type=F32)
        acc = part if acc is None else acc + part
    o_ref[...] = x + 0.5 * _rms(acc, g_ref[1:2, :])


def _ffn(x, gains2, wg, wu, wd):
    n = x.shape[0]
    tm = _token_tile(n)
    return pl.pallas_call(
        _ffn_body,
        out_shape=jax.ShapeDtypeStruct((n, D_MODEL), F32),
        grid=(n // tm,),
        in_specs=[
            pl.BlockSpec((tm, D_MODEL), lambda i: (i, 0)),
            _resident((2, D_MODEL)),
            _resident((D_MODEL, D_FF)),
            _resident((D_MODEL, D_FF)),
            _resident((D_FF, D_MODEL)),
        ],
        out_specs=pl.BlockSpec((tm, D_MODEL), lambda i: (i, 0)),
        compiler_params=pltpu.CompilerParams(dimension_semantics=("arbitrary",), vmem_limit_bytes=VMEM_LIMIT),
        name="ffn",
    )(x, gains2, wg, wu, wd)


_C_Q = 0
_C_F = HG_WIDTH
_C_I = 2 * HG_WIDTH
_C_G = 3 * HG_WIDTH
_C_Z = 4 * HG_WIDTH
_C_X = _C_Z + M_DINNER


def _inproj_body(layer, x_ref, g_ref, lbl_ref, dtb_ref, w_ref, wdt_ref,
                 q_ref, k_ref, gl_ref, v_ref, go_ref, z_ref, xbc_ref, dt_ref):
    xn = _rms(x_ref[...], g_ref[...]).astype(BF16)

    def proj(c0, width):
        return jnp.dot(xn, w_ref[:, c0:c0 + width].astype(BF16), preferred_element_type=F32)

    lg = lbl_ref[...]
    e = jnp.exp(lg - jnp.max(lg, axis=0, keepdims=True))
    p = e / jnp.sum(e, axis=0, keepdims=True)
    lb = jnp.sum(p[0:layer + 1, :], axis=0, keepdims=True) - p[0:1, :]
    log_lb = jnp.log(jnp.maximum(lb, LB_FLOOR))
    log_1mlb = jnp.log1p(-lb)

    q = proj(_C_Q, HG_WIDTH) * (HG_KDIM ** -0.5)
    fr = proj(_C_F, HG_WIDTH)
    glog = jnp.logaddexp(log_lb, log_1mlb + jax.nn.log_sigmoid(fr))
    kk = (1.0 - lb) * jax.nn.sigmoid(-fr)
    v = proj(_C_I, HG_WIDTH)
    go = _silu(proj(_C_G, HG_WIDTH))
    for h in range(HG_HEADS):
        sl = slice(h * HG_KDIM, (h + 1) * HG_KDIM)
        q_ref[h] = q[:, sl]
        k_ref[h] = kk[:, sl]
        gl_ref[h] = glog[:, sl]
        v_ref[h] = v[:, sl]
        go_ref[h] = go[:, sl]
    z_ref[...] = _silu(proj(_C_Z, M_DINNER))
    xbc_ref[...] = proj(_C_X, CONV_DIM)
    dt_ref[...] = jax.nn.softplus(jnp.dot(xn, wdt_ref[...].astype(BF16), preferred_element_type=F32) + dtb_ref[...])


def _inproj(h, gain, lb_logits, dt_bias_pad, w, w_dt, layer):
    n = h.shape[0]
    tm = _token_tile(n, largest=INPROJ_TILE)
    head_major = jax.ShapeDtypeStruct((HG_HEADS, n, HG_KDIM), F32)
    head_spec = pl.BlockSpec((HG_HEADS, tm, HG_KDIM), lambda i: (0, i, 0))

    def tok(width):
        return pl.BlockSpec((tm, width), lambda i: (i, 0))

    return pl.pallas_call(
        functools.partial(_inproj_body, layer),
        out_shape=(head_major,) * 5 + (
            jax.ShapeDtypeStruct((n, M_DINNER), F32),
            jax.ShapeDtypeStruct((n, CONV_DIM), F32),
            jax.ShapeDtypeStruct((n, DT_PAD), F32),
        ),
        grid=(n // tm,),
        in_specs=[
            tok(D_MODEL),
            _resident((1, D_MODEL)),
            _resident((DEPTH, HG_WIDTH)),
            _resident((1, DT_PAD)),
            pl.BlockSpec((None,) + w.shape[1:], lambda i: (layer, 0, 0), pipeline_mode=pl.Buffered(1)),
            _resident((D_MODEL, DT_PAD)),
        ],
        out_specs=(head_spec,) * 5 + (tok(M_DINNER), tok(CONV_DIM), tok(DT_PAD)),
        compiler_params=pltpu.CompilerParams(dimension_semantics=("arbitrary",), vmem_limit_bytes=VMEM_LIMIT),
        name="inproj",
    )(h, gain, lb_logits, dt_bias_pad, w, w_dt)


DIAG = SUBLANES


def _expand_heads(vals, group):
    rows = vals.shape[0]
    lane = lax.broadcasted_iota(jnp.int32, (rows, LANES), 1)
    tiles = []
    for j in range(M_GROUP_WIDTH // LANES):
        h0 = group * (M_HEADS // M_GROUPS) + 2 * j
        lo = jnp.broadcast_to(vals[:, h0:h0 + 1], (rows, LANES))
        hi = jnp.broadcast_to(vals[:, h0 + 1:h0 + 2], (rows, LANES))
        tiles.append(jnp.where(lane < M_HEADDIM, lo, hi))
    return jnp.concatenate(tiles, axis=1)


_MIXER_INPUTS = 17


def _mixer_body(nseq, seq_chunk, *refs):
    (q_ref, k_ref, gl_ref, v_ref, go_ref, z_ref, xbc_ref, dt_ref, hg0_ref, ssm0_ref, conv0_ref,
     convw_ref, convb_ref, alog_ref, dskip_ref, hgw_ref, ssmw_ref) = refs[:_MIXER_INPUTS]
    ohg_ref, yssm_ref, hg_out_ref, ssm_out_ref, conv_out_ref, st_hg, st_ssm, cat_ref = refs[-8:]
    chunk = nseq * seq_chunk
    step = pl.program_id(1)
    last_step = pl.num_programs(1) - 1
    heads_per_group = M_HEADS // M_GROUPS

    @pl.when(step == 0)
    def _load_states():
        for i in range(nseq):
            for h in range(HG_HEADS):
                st_hg[i, h] = hg0_ref[i, h].T
            for g in range(M_GROUPS):
                s0 = ssm0_ref[i, g * heads_per_group:(g + 1) * heads_per_group]
                st_ssm[i, g] = s0.reshape(M_GROUP_WIDTH, M_DSTATE).T
            cat_ref[i, 0:SUBLANES, :] = conv0_ref[i]

    row = lax.broadcasted_iota(jnp.int32, (chunk, chunk), 0)
    col = lax.broadcasted_iota(jnp.int32, (chunk, chunk), 1)
    causal = (col <= row) & ((row // seq_chunk) == (col // seq_chunk))
    causal_f = causal.astype(F32)
    row1 = lax.broadcasted_iota(jnp.int32, (chunk, 1), 0)
    lane_c = lax.broadcasted_iota(jnp.int32, (DIAG, chunk), 1)
    sub_k = lax.broadcasted_iota(jnp.int32, (DIAG, HG_KDIM), 0)
    seq_rows = [slice(i * seq_chunk, (i + 1) * seq_chunk) for i in range(nseq)]

    for h in range(HG_HEADS):
        q = q_ref[h]
        k = k_ref[h]
        v = v_ref[h]
        b = jnp.dot(causal_f, gl_ref[h], precision=HIGHEST, preferred_element_type=F32)
        qe = q * jnp.exp(b)
        o = jnp.concatenate([_mm_nt(qe[r], st_hg[i, h]) for i, r in enumerate(seq_rows)], axis=0)

        scores = jnp.zeros((chunk, chunk), F32)
        m = DIAG
        while 2 * m <= seq_chunk:
            blk = 2 * m
            bref = jnp.concatenate(
                [jnp.broadcast_to(b[j * blk + m - 1:j * blk + m, :], (blk, HG_KDIM)) for j in range(chunk // blk)],
                axis=0)
            upper = (row1 % blk) >= m
            e = jnp.exp(jnp.where(upper, b - bref, bref - b))
            qt = jnp.where(upper, q * e, 0.0)
            kt = jnp.where(upper, 0.0, k * e)
            scores = scores + jnp.where((row // blk) == (col // blk), _mm_nt(qt, kt), 0.0)
            m = blk

        tiles = []
        for sb in range(chunk // DIAG):
            r0 = sb * DIAG
            qs = q[r0:r0 + DIAG]
            bs = b[r0:r0 + DIAG]
            tile = jnp.zeros((DIAG, chunk), F32)
            for j in range(DIAG):
                s = r0 + j
                e = jnp.exp(jnp.where(sub_k >= j, bs - b[s:s + 1, :], NEG_BIG))
                colv = jnp.sum(qs * k[s:s + 1, :] * e, axis=-1, keepdims=True)
                tile = jnp.where(lane_c == s, colv, tile)
            tiles.append(tile)
        scores = scores + jnp.concatenate(tiles, axis=0)
        o = o + _mm(scores, v)

        for i, r in enumerate(seq_rows):
            b_last = b[r.stop - 1:r.stop, :]
            kd = k[r] * jnp.exp(b_last - b[r])
            st_hg[i, h] = jnp.exp(b_last) * st_hg[i, h] + _mm_tn(v[r], kd)

        on = _rms(o, hgw_ref[h]) * go_ref[h]
        ohg_ref[:, h * HG_VDIM:(h + 1) * HG_VDIM] = on.astype(BF16)

    acts = []
    for i, r in enumerate(seq_rows):
        cat_ref[i, SUBLANES:SUBLANES + seq_chunk, :] = xbc_ref[r, :]
        conv = convb_ref[...]
        for j in range(CONV_W):
            off = SUBLANES - (CONV_W - 1) + j
            conv = conv + cat_ref[i, off:off + seq_chunk, :] * convw_ref[j:j + 1, :]
        acts.append(_silu(conv))
        cat_ref[i, 0:SUBLANES, :] = cat_ref[i, seq_chunk:seq_chunk + SUBLANES, :]
    act = acts[0] if nseq == 1 else jnp.concatenate(acts, axis=0)
    xs = act[:, 0:M_DINNER]

    lane_t = lax.broadcasted_iota(jnp.int32, (chunk, DT_PAD), 1)
    dt = jnp.where(lane_t < M_HEADS, dt_ref[...], 0.0)
    da = dt * (-jnp.exp(alog_ref[...]))
    cum = jnp.dot(causal_f, da, precision=HIGHEST, preferred_element_type=F32)
    cum_t = cum.T
    dt_t = dt.T
    ecum = jnp.exp(cum)
    lane_x = lax.broadcasted_iota(jnp.int32, (chunk, LANES), 1)

    y_groups = []
    for g in range(M_GROUPS):
        b_g = act[:, M_DINNER + g * M_DSTATE:M_DINNER + (g + 1) * M_DSTATE]
        c_g = act[:, M_DINNER + (M_GROUPS + g) * M_DSTATE:M_DINNER + (M_GROUPS + g + 1) * M_DSTATE]
        x_g = xs[:, g * M_GROUP_WIDTH:(g + 1) * M_GROUP_WIDTH]
        cb = _mm_nt(c_g, b_g)
        y_tiles = []
        for j in range(M_GROUP_WIDTH // LANES):
            x_pair = x_g[:, j * LANES:(j + 1) * LANES]
            y_pair = None
            for half in range(2):
                hd = g * heads_per_group + 2 * j + half
                diff = jnp.broadcast_to(cum[:, hd:hd + 1], (chunk, chunk)) - jnp.broadcast_to(cum_t[hd:hd + 1, :], (chunk, chunk))
                decay = jnp.exp(jnp.where(causal, diff, NEG_BIG))
                mh = cb * decay * jnp.broadcast_to(dt_t[hd:hd + 1, :], (chunk, chunk))
                keep = (lane_x < M_HEADDIM) if half == 0 else (lane_x >= M_HEADDIM)
                part = _mm(mh, jnp.where(keep, x_pair, 0.0))
                y_pair = part if y_pair is None else y_pair + part
            y_tiles.append(y_pair)
        y_intra = jnp.concatenate(y_tiles, axis=1)
        y_inter = jnp.concatenate([_mm(c_g[r], st_ssm[i, g]) for i, r in enumerate(seq_rows)], axis=0)
        y_g = y_intra + y_inter * _expand_heads(ecum, g)
        y_g = y_g + dskip_ref[:, g * M_GROUP_WIDTH:(g + 1) * M_GROUP_WIDTH] * x_g
        y_g = y_g * z_ref[:, g * M_GROUP_WIDTH:(g + 1) * M_GROUP_WIDTH]
        y_groups.append(_rms(y_g, ssmw_ref[:, g * M_GROUP_WIDTH:(g + 1) * M_GROUP_WIDTH]))

        for i, r in enumerate(seq_rows):
            cum_last = cum[r.stop - 1:r.stop, :]
            wgt = dt[r] * jnp.exp(cum_last - cum[r])
            xw = x_g[r] * _expand_heads(wgt, g)
            st_ssm[i, g] = _expand_heads(jnp.exp(cum_last), g) * st_ssm[i, g] + _mm_tn(b_g[r], xw)
    yssm_ref[...] = jnp.concatenate(y_groups, axis=1).astype(BF16)

    @pl.when(step == last_step)
    def _store_states():
        for i in range(nseq):
            for h in range(HG_HEADS):
                hg_out_ref[i, h] = st_hg[i, h].T
            for g in range(M_GROUPS):
                ssm_out_ref[i, g * heads_per_group:(g + 1) * heads_per_group] = (
                    st_ssm[i, g].T.reshape(heads_per_group, M_HEADDIM, M_DSTATE))
            conv_out_ref[i] = cat_ref[i, 0:SUBLANES, :]


def _mixer(heads5, zs, xbc, dt, hg0, ssm0, conv0, convw, convb, alog, dskip, hgw, ssmw, prev_out, *,
           layer, batch, seq_len):
    if seq_len >= LANES:
        nseq, seq_chunk = 1, LANES
    else:
        nseq, seq_chunk = SAMPLE_SEQS_PER_STEP, seq_len
    assert seq_len % seq_chunk == 0 and batch % nseq == 0 and seq_chunk % DIAG == 0
    chunk = nseq * seq_chunk
    steps = seq_len // seq_chunk
    n = batch * seq_len

    def tok_idx(b, s):
        return b * steps + s

    head_spec = pl.BlockSpec((HG_HEADS, chunk, HG_KDIM), lambda b, s: (0, tok_idx(b, s), 0))

    def tok(width):
        return pl.BlockSpec((chunk, width), lambda b, s: (tok_idx(b, s), 0))

    hg_spec = pl.BlockSpec((None, nseq, HG_HEADS, HG_KDIM, HG_VDIM), lambda b, s: (layer, b, 0, 0, 0))
    ssm_spec = pl.BlockSpec((None, nseq, M_HEADS, M_HEADDIM, M_DSTATE), lambda b, s: (layer, b, 0, 0, 0))
    conv_spec = pl.BlockSpec((None, nseq, SUBLANES, CONV_DIM), lambda b, s: (layer, b, 0, 0))
    prev_out = () if prev_out is None else tuple(prev_out)
    first_state_out = 2
    assert len(prev_out) in (0, 3)
    aliases = {_MIXER_INPUTS + j: first_state_out + j for j in range(len(prev_out))}

    return pl.pallas_call(
        functools.partial(_mixer_body, nseq, seq_chunk),
        out_shape=(
            jax.ShapeDtypeStruct((n, HG_WIDTH), BF16),
            jax.ShapeDtypeStruct((n, M_DINNER), BF16),
            jax.ShapeDtypeStruct(hg0.shape, F32),
            jax.ShapeDtypeStruct(ssm0.shape, F32),
            jax.ShapeDtypeStruct(conv0.shape, F32),
        ),
        grid=(batch // nseq, steps),
        in_specs=[head_spec] * 5 + [tok(M_DINNER), tok(CONV_DIM), tok(DT_PAD), hg_spec, ssm_spec, conv_spec,
                                    _resident((CONV_W, CONV_DIM)), _resident((1, CONV_DIM)), _resident((1, DT_PAD)),
                                    _resident((1, M_DINNER)), _resident((HG_HEADS, 1, HG_VDIM)), _resident((1, M_DINNER))]
        + [pl.BlockSpec(memory_space=pl.ANY)] * len(prev_out),
        out_specs=(tok(HG_WIDTH), tok(M_DINNER), hg_spec, ssm_spec, conv_spec),
        input_output_aliases=aliases,
        scratch_shapes=[
            pltpu.VMEM((nseq, HG_HEADS, HG_VDIM, HG_KDIM), F32),
            pltpu.VMEM((nseq, M_GROUPS, M_DSTATE, M_GROUP_WIDTH), F32),
            pltpu.VMEM((nseq, SUBLANES + seq_chunk, CONV_DIM), F32),
        ],
        compiler_params=pltpu.CompilerParams(dimension_semantics=("arbitrary", "arbitrary"),
                                             vmem_limit_bytes=VMEM_LIMIT),
        name="mixer",
    )(*heads5, zs, xbc, dt, hg0, ssm0, conv0, convw, convb, alog, dskip, hgw, ssmw, *prev_out)


def _outproj_body(h_ref, ohg_ref, yssm_ref, g_ref, w_ref, o_ref):
    m = jnp.dot(ohg_ref[...], w_ref[0:HG_WIDTH, :], preferred_element_type=F32)
    m = m + jnp.dot(yssm_ref[...], w_ref[HG_WIDTH:HG_WIDTH + M_DINNER, :], preferred_element_type=F32)
    o_ref[...] = h_ref[...] + _rms(m, g_ref[...])


def _outproj(h, ohg, yssm, gain, w):
    n = h.shape[0]
    tm = _token_tile(n)

    def tok(width):
        return pl.BlockSpec((tm, width), lambda i: (i, 0))

    return pl.pallas_call(
        _outproj_body,
        out_shape=jax.ShapeDtypeStruct((n, D_MODEL), F32),
        grid=(n // tm,),
        in_specs=[tok(D_MODEL), tok(HG_WIDTH), tok(M_DINNER), _resident((1, D_MODEL)),
                  _resident((HG_WIDTH + M_DINNER, D_MODEL))],
        out_specs=tok(D_MODEL),
        compiler_params=pltpu.CompilerParams(dimension_semantics=("arbitrary",), vmem_limit_bytes=VMEM_LIMIT),
        name="outproj",
    )(h, ohg, yssm, gain, w)


def _prep_layer(l, w_in, hg_lb_logits, conv_w, conv_b, dt_bias, a_log, d_skip, hg_norm_w, ssm_norm_w, w_out,
                f1g, f1u, f1d, f2g, f2u, f2d, norm_gain):
    pad = DT_PAD - M_HEADS
    return dict(
        layer=l,
        w_in=w_in,
        w_dt=jnp.pad(w_in[l][:, D_PROJ_MAIN:], ((0, 0), (0, pad))),
        lb_logits=hg_lb_logits,
        conv_w=conv_w[l],
        conv_b=conv_b[l][None, :],
        dt_bias=jnp.pad(dt_bias[l], (0, pad))[None, :],
        a_log=jnp.pad(a_log[l], (0, pad))[None, :],
        d_skip=jnp.repeat(d_skip[l], M_HEADDIM)[None, :],
        hg_norm_w=hg_norm_w[l].reshape(HG_HEADS, 1, HG_VDIM),
        ssm_norm_w=ssm_norm_w[l][None, :],
        w_out=w_out[l].astype(BF16),
        ffn1=(f1g[l].astype(BF16), f1u[l].astype(BF16), f1d[l].astype(BF16)),
        ffn2=(f2g[l].astype(BF16), f2u[l].astype(BF16), f2d[l].astype(BF16)),
        gains=norm_gain[l],
    )


def _layer(x, hg0, ssm0, conv0, prev_out, p, *, batch, seq_len):
    gains = p["gains"]
    h = _ffn(x, gains[0:2], *p["ffn1"])
    *heads5, zs, xbc, dt = _inproj(h, gains[2:3], p["lb_logits"], p["dt_bias"], p["w_in"], p["w_dt"], p["layer"])
    ohg, yssm, *states = _mixer(
        heads5, zs, xbc, dt, hg0, ssm0, conv0, p["conv_w"], p["conv_b"], p["a_log"], p["d_skip"],
        p["hg_norm_w"], p["ssm_norm_w"], prev_out, layer=p["layer"], batch=batch, seq_len=seq_len)
    h = _outproj(h, ohg, yssm, gains[3:4], p["w_out"])
    h = _ffn(h, gains[4:6], *p["ffn2"])
    return h, states


def _trunk(x, s_hg, s_ssm, s_conv, layers):
    batch, seq_len, _ = x.shape
    h = x.reshape(batch * seq_len, D_MODEL)
    conv_pad = jnp.pad(s_conv, ((0, 0), (0, 0), (SUBLANES - (CONV_W - 1), 0), (0, 0)))
    states = None
    for p in layers:
        h, states = _layer(h, s_hg, s_ssm, conv_pad, states, p, batch=batch, seq_len=seq_len)
    hg_out, ssm_out, conv_out = states
    return h.reshape(batch, seq_len, D_MODEL), hg_out, ssm_out, conv_out[:, :, SUBLANES - (CONV_W - 1):, :]


def kernel(x_prompt, x_sample, state_hgrn, state_ssm, state_conv, w_in, hg_lb_logits, conv_w, conv_b, dt_bias, a_log, d_skip, hg_norm_w, ssm_norm_w, w_out, ffn1_w_gate, ffn1_w_up, ffn1_w_down, ffn2_w_gate, ffn2_w_up, ffn2_w_down, norm_gain):
    layers = [
        _prep_layer(l, w_in, hg_lb_logits, conv_w, conv_b, dt_bias, a_log, d_skip, hg_norm_w, ssm_norm_w, w_out,
                    ffn1_w_gate, ffn1_w_up, ffn1_w_down, ffn2_w_gate, ffn2_w_up, ffn2_w_down, norm_gain)
        for l in range(DEPTH)
    ]
    bp = x_prompt.shape[0]
    y_prompt, hg_p, ssm_p, conv_p = _trunk(
        x_prompt,
        jnp.zeros((DEPTH, bp, HG_HEADS, HG_KDIM, HG_VDIM), F32),
        jnp.zeros((DEPTH, bp, M_HEADS, M_HEADDIM, M_DSTATE), F32),
        jnp.zeros((DEPTH, bp, CONV_W - 1, CONV_DIM), F32),
        layers)
    y_sample, hg_s, ssm_s, conv_s = _trunk(x_sample, state_hgrn, state_ssm, state_conv, layers)
    return (y_prompt, y_sample, hg_p, ssm_p, conv_p, hg_s, ssm_s, conv_s)
```

```python
import functools

import jax
import jax.numpy as jnp
from jax import lax
from jax.experimental import pallas as pl
from jax.experimental.pallas import tpu as pltpu

F32 = jnp.float32
BF16 = jnp.bfloat16

D_MODEL = 1024
DEPTH = 2
HG_HEADS = 8
HG_KDIM = 128
HG_VDIM = 128
HG_WIDTH = HG_HEADS * HG_KDIM
LB_FLOOR = 1e-30
M_DINNER = 1024
M_HEADDIM = 64
M_HEADS = 16
M_DSTATE = 128
M_GROUPS = 2
M_GROUP_WIDTH = M_DINNER // M_GROUPS
CONV_W = 4
CONV_DIM = M_DINNER + 2 * M_GROUPS * M_DSTATE
D_FF = 2816
EPS = 1e-6

LANES = 128
SUBLANES = 8
DT_PAD = LANES
D_PROJ_MAIN = 4 * HG_WIDTH + M_DINNER + CONV_DIM
INPROJ_TILE = 256
SAMPLE_SEQS_PER_STEP = 4
LOG2E = 1.4426950408889634
NEG_BIG = -1e30
VMEM_LIMIT = 56 * 1024 * 1024

_NT = (((1,), (1,)), ((), ()))
_TN = (((0,), (0,)), ((), ()))


def _rms(x, w):
    return x * lax.rsqrt(jnp.mean(x * x, axis=-1, keepdims=True) + EPS) * w


def _silu(x):
    return x * jax.nn.sigmoid(x)


def _mm(a, b):
    return jnp.dot(a.astype(BF16), b.astype(BF16), preferred_element_type=F32)


def _mm_nt(a, b):
    return lax.dot_general(a.astype(BF16), b.astype(BF16), _NT, preferred_element_type=F32)


def _mm_tn(a, b):
    return lax.dot_general(a.astype(BF16), b.astype(BF16), _TN, preferred_element_type=F32)


def _resident(shape):
    nd = len(shape)
    return pl.BlockSpec(shape, lambda *_: (0,) * nd, pipeline_mode=pl.Buffered(1))


def _resident_layer(stacked, layer):
    return pl.BlockSpec((None,) + stacked.shape[1:], lambda *_: (layer, 0, 0), pipeline_mode=pl.Buffered(1))


def _token_tile(n_tokens, largest=512):
    for tm in (t for t in (512, 256, 128, 64, 32, 16, 8) if t <= largest):
        if n_tokens % tm == 0:
            return tm
    raise ValueError(f"token count {n_tokens} is not a multiple of 8")


FF_CHUNK = 256


def _ffn_body(x_ref, g_ref, wg_ref, wu_ref, wd_ref, o_ref):
    x = x_ref[...]
    xn = _rms(x, g_ref[0:1, :]).astype(BF16)
    acc = None
    for j in range(D_FF // FF_CHUNK):
        sl = slice(j * FF_CHUNK, (j + 1) * FF_CHUNK)
        gate = jnp.dot(xn, wg_ref[:, sl], preferred_element_type=F32)
        up = jnp.dot(xn, wu_ref[:, sl], preferred_element_type=F32)
        act = (_silu(gate) * up).astype(BF16)
        part = jnp.dot(act, wd_ref[sl, :], preferred_element_type=F32)
        acc = part if acc is None else acc + part
    o_ref[...] = x + 0.5 * _rms(acc, g_ref[1:2, :])


def _ffn(x, gains2, wg, wu, wd, layer):
    n = x.shape[0]
    tm = _token_tile(n)
    return pl.pallas_call(
        _ffn_body,
        out_shape=jax.ShapeDtypeStruct((n, D_MODEL), F32),
        grid=(n // tm,),
        in_specs=[
            pl.BlockSpec((tm, D_MODEL), lambda i: (i, 0)),
            _resident((2, D_MODEL)),
            _resident_layer(wg, layer),
            _resident_layer(wu, layer),
            _resident_layer(wd, layer),
        ],
        out_specs=pl.BlockSpec((tm, D_MODEL), lambda i: (i, 0)),
        compiler_params=pltpu.CompilerParams(dimension_semantics=("arbitrary",), vmem_limit_bytes=VMEM_LIMIT),
        name="ffn",
    )(x, gains2, wg, wu, wd)


_C_Q = 0
_C_F = HG_WIDTH
_C_I = 2 * HG_WIDTH
_C_G = 3 * HG_WIDTH
_C_Z = 4 * HG_WIDTH
_C_X = _C_Z + M_DINNER


def _inproj_body(layer, x_ref, g_ref, lbl_ref, dtb_ref, w_ref, wdt_ref,
                 q_ref, k_ref, gl_ref, v_ref, go_ref, z_ref, xbc_ref, dt_ref):
    xn = _rms(x_ref[...], g_ref[...]).astype(BF16)

    def proj(c0, width):
        return jnp.dot(xn, w_ref[:, c0:c0 + width].astype(BF16), preferred_element_type=F32)

    lg = lbl_ref[...]
    e = jnp.exp(lg - jnp.max(lg, axis=0, keepdims=True))
    p = e / jnp.sum(e, axis=0, keepdims=True)
    lb = jnp.sum(p[0:layer + 1, :], axis=0, keepdims=True) - p[0:1, :]
    log_lb = jnp.log(jnp.maximum(lb, LB_FLOOR))
    log_1mlb = jnp.log1p(-lb)

    q = proj(_C_Q, HG_WIDTH) * (HG_KDIM ** -0.5)
    fr = proj(_C_F, HG_WIDTH)
    glog = jnp.logaddexp(log_lb, log_1mlb + jax.nn.log_sigmoid(fr))
    kk = (1.0 - lb) * jax.nn.sigmoid(-fr)
    v = proj(_C_I, HG_WIDTH)
    go = _silu(proj(_C_G, HG_WIDTH))
    for h in range(HG_HEADS):
        sl = slice(h * HG_KDIM, (h + 1) * HG_KDIM)
        q_ref[h] = q[:, sl]
        k_ref[h] = kk[:, sl]
        gl_ref[h] = glog[:, sl] * LOG2E
        v_ref[h] = v[:, sl]
        go_ref[h] = go[:, sl]
    z_ref[...] = _silu(proj(_C_Z, M_DINNER))
    xbc_ref[...] = proj(_C_X, CONV_DIM)
    dt_ref[...] = jax.nn.softplus(jnp.dot(xn, wdt_ref[...].astype(BF16), preferred_element_type=F32) + dtb_ref[...])


def _inproj(h, gain, lb_logits, dt_bias_pad, w, w_dt, layer):
    n = h.shape[0]
    tm = _token_tile(n, largest=INPROJ_TILE)
    head_major = jax.ShapeDtypeStruct((HG_HEADS, n, HG_KDIM), F32)
    head_spec = pl.BlockSpec((HG_HEADS, tm, HG_KDIM), lambda i: (0, i, 0))

    def tok(width):
        return pl.BlockSpec((tm, width), lambda i: (i, 0))

    return pl.pallas_call(
        functools.partial(_inproj_body, layer),
        out_shape=(head_major,) * 5 + (
            jax.ShapeDtypeStruct((n, M_DINNER), F32),
            jax.ShapeDtypeStruct((n, CONV_DIM), F32),
            jax.ShapeDtypeStruct((n, DT_PAD), F32),
        ),
        grid=(n // tm,),
        in_specs=[
            tok(D_MODEL),
            _resident((1, D_MODEL)),
            _resident((DEPTH, HG_WIDTH)),
            _resident((1, DT_PAD)),
            _resident_layer(w, layer),
            _resident((D_MODEL, DT_PAD)),
        ],
        out_specs=(head_spec,) * 5 + (tok(M_DINNER), tok(CONV_DIM), tok(DT_PAD)),
        compiler_params=pltpu.CompilerParams(dimension_semantics=("arbitrary",), vmem_limit_bytes=VMEM_LIMIT),
        name="inproj",
    )(h, gain, lb_logits, dt_bias_pad, w, w_dt)


BAND = 4


def _expand_heads(vals, group):
    rows = vals.shape[0]
    lane = lax.broadcasted_iota(jnp.int32, (rows, LANES), 1)
    tiles = []
    for j in range(M_GROUP_WIDTH // LANES):
        h0 = group * (M_HEADS // M_GROUPS) + 2 * j
        lo = jnp.broadcast_to(vals[:, h0:h0 + 1], (rows, LANES))
        hi = jnp.broadcast_to(vals[:, h0 + 1:h0 + 2], (rows, LANES))
        tiles.append(jnp.where(lane < M_HEADDIM, lo, hi))
    return jnp.concatenate(tiles, axis=1)


def _as_column(row_vec):
    n = row_vec.shape[1]
    hi = row_vec.astype(BF16).astype(F32)
    mid = (row_vec - hi).astype(BF16).astype(F32)
    lo = row_vec - hi - mid
    sub = lax.broadcasted_iota(jnp.int32, (SUBLANES, n), 0)
    pieces = jnp.where(sub == 0, hi, jnp.where(sub == 1, mid, jnp.where(sub == 2, lo, 0.0)))
    return _mm_tn(pieces, jnp.ones((SUBLANES, n), F32))


_MIXER_INPUTS = 17


def _mixer_body(nseq, seq_chunk, *refs):
    (q_ref, k_ref, gl_ref, v_ref, go_ref, z_ref, xbc_ref, dt_ref, hg0_ref, ssm0_ref, conv0_ref,
     convw_ref, convb_ref, alog_ref, dskip_ref, hgw_ref, ssmw_ref) = refs[:_MIXER_INPUTS]
    ohg_ref, yssm_ref, hg_out_ref, ssm_out_ref, conv_out_ref, st_hg, st_ssm, cat_ref, shift_ref = refs[-9:]
    chunk = nseq * seq_chunk
    step = pl.program_id(1)
    last_step = pl.num_programs(1) - 1
    heads_per_group = M_HEADS // M_GROUPS

    @pl.when(step == 0)
    def _load_states():
        for i in range(nseq):
            for h in range(HG_HEADS):
                st_hg[i, h] = hg0_ref[i, h]
            for g in range(M_GROUPS):
                s0 = ssm0_ref[i, g * heads_per_group:(g + 1) * heads_per_group]
                st_ssm[i, g] = s0.reshape(M_GROUP_WIDTH, M_DSTATE)
            for c in range(CONV_DIM // LANES):
                cat_ref[i, c, 0:SUBLANES, :] = conv0_ref[i, :, c * LANES:(c + 1) * LANES]
        shift_ref[:, :, 0:SUBLANES, :] = jnp.zeros((HG_HEADS, 2, SUBLANES, HG_KDIM), F32)

    row = lax.broadcasted_iota(jnp.int32, (chunk, chunk), 0)
    col = lax.broadcasted_iota(jnp.int32, (chunk, chunk), 1)
    causal = (col <= row) & ((row // seq_chunk) == (col // seq_chunk))
    causal_bf = causal.astype(F32).astype(BF16)
    seq_rows = [slice(i * seq_chunk, (i + 1) * seq_chunk) for i in range(nseq)]

    def cumsum_rows(x):
        hi = x.astype(BF16)
        rest = x - hi.astype(F32)
        mid = rest.astype(BF16)
        lo = (rest - mid.astype(F32)).astype(BF16)
        return sum(jnp.dot(causal_bf, piece, preferred_element_type=F32) for piece in (hi, mid, lo))

    levels = []
    m = BAND
    while 2 * m <= seq_chunk:
        levels.append(m)
        m *= 2
    pair_class = jnp.full((chunk, chunk), -1, jnp.int32)
    for lvl in reversed(range(len(levels))):
        blk_log2 = jnp.int32((2 * levels[lvl]).bit_length() - 1)
        same_blk = lax.shift_right_logical(row, blk_log2) == lax.shift_right_logical(col, blk_log2)
        pair_class = jnp.where(same_blk, BAND + lvl, pair_class)
    pair_class = jnp.where(row - col < BAND, row - col, pair_class)
    pair_class = jnp.where(causal, pair_class, -1)

    for h in range(HG_HEADS):
        shift_ref[h, 0, SUBLANES:SUBLANES + chunk, :] = cumsum_rows(gl_ref[h])
        shift_ref[h, 1, SUBLANES:SUBLANES + chunk, :] = k_ref[h]

    def gla_head(h):
        q = q_ref[h]
        k = k_ref[h]
        v = v_ref[h]
        b = shift_ref[h, 0, SUBLANES:SUBLANES + chunk, :]
        qe = q * jnp.exp2(b)
        o = jnp.concatenate([_mm(qe[r], st_hg[i, h]) for i, r in enumerate(seq_rows)], axis=0)

        scores = jnp.zeros((chunk, chunk), F32)
        for dist in range(BAND):
            if dist == 0:
                p = jnp.sum(q * k, axis=-1, keepdims=True)
            else:
                lo = SUBLANES - dist
                d = b - shift_ref[h, 0, lo:lo + chunk, :]
                if nseq > 1:
                    d = jnp.minimum(d, 0.0)
                p = jnp.sum(q * shift_ref[h, 1, lo:lo + chunk, :] * jnp.exp2(d), axis=-1, keepdims=True)
            scores = jnp.where(pair_class == dist, p, scores)

        for lvl, m in enumerate(levels):
            blk = 2 * m
            bref = jnp.concatenate(
                [jnp.broadcast_to(b[j * blk + m - 1:j * blk + m, :], (blk, HG_KDIM)) for j in range(chunk // blk)],
                axis=0)
            e = jnp.exp2(-jnp.abs(b - bref))
            scores = jnp.where(pair_class == BAND + lvl, _mm_nt(q * e, k * e), scores)
        o = o + _mm(scores, v)

        for i, r in enumerate(seq_rows):
            b_last = b[r.stop - 1:r.stop, :]
            kd = k[r] * jnp.exp2(b_last - b[r])
            st_hg[i, h] = _as_column(jnp.exp2(b_last)) * st_hg[i, h] + _mm_tn(kd, v[r])

        on = _rms(o, hgw_ref[h]) * go_ref[h]
        ohg_ref[:, h * HG_VDIM:(h + 1) * HG_VDIM] = on.astype(BF16)

    acts = []
    for i, r in enumerate(seq_rows):
        slabs = []
        for c in range(CONV_DIM // LANES):
            lanes = slice(c * LANES, (c + 1) * LANES)
            cat_ref[i, c, SUBLANES:SUBLANES + seq_chunk, :] = xbc_ref[r, lanes]
            conv = convb_ref[:, lanes]
            for j in range(CONV_W):
                off = SUBLANES - (CONV_W - 1) + j
                conv = conv + cat_ref[i, c, off:off + seq_chunk, :] * convw_ref[j:j + 1, lanes]
            slabs.append(_silu(conv))
            cat_ref[i, c, 0:SUBLANES, :] = cat_ref[i, c, seq_chunk:seq_chunk + SUBLANES, :]
        acts.append(jnp.concatenate(slabs, axis=1))
    act = acts[0] if nseq == 1 else jnp.concatenate(acts, axis=0)
    xs = act[:, 0:M_DINNER]

    lane_t = lax.broadcasted_iota(jnp.int32, (chunk, DT_PAD), 1)
    dt = jnp.where(lane_t < M_HEADS, dt_ref[...], 0.0)
    da = dt * (-LOG2E * jnp.exp(alog_ref[...]))
    cum = cumsum_rows(da)
    key_term = cum.T[0:M_HEADS, :] - jnp.log2(dt.T[0:M_HEADS, :])
    ecum = jnp.exp2(cum)
    lane_x = lax.broadcasted_iota(jnp.int32, (chunk, LANES), 1)
    pairs_per_group = M_GROUP_WIDTH // LANES
    assert HG_HEADS == M_GROUPS * pairs_per_group

    def group_operands(g):
        b_g = act[:, M_DINNER + g * M_DSTATE:M_DINNER + (g + 1) * M_DSTATE]
        c_g = act[:, M_DINNER + (M_GROUPS + g) * M_DSTATE:M_DINNER + (M_GROUPS + g + 1) * M_DSTATE]
        return b_g, c_g, xs[:, g * M_GROUP_WIDTH:(g + 1) * M_GROUP_WIDTH]

    cb_groups = [_mm_nt(group_operands(g)[1], group_operands(g)[0]) for g in range(M_GROUPS)]

    def ssd_pair(g, j):
        x_pair = group_operands(g)[2][:, j * LANES:(j + 1) * LANES]
        y_pair = None
        for half in range(2):
            hd = g * heads_per_group + 2 * j + half
            diff = (jnp.broadcast_to(cum[:, hd:hd + 1], (chunk, chunk))
                    - jnp.broadcast_to(key_term[hd:hd + 1, :], (chunk, chunk)))
            mh = cb_groups[g] * jnp.exp2(jnp.where(causal, diff, NEG_BIG))
            keep = (lane_x < M_HEADDIM) if half == 0 else (lane_x >= M_HEADDIM)
            part = _mm(mh, jnp.where(keep, x_pair, 0.0))
            y_pair = part if y_pair is None else y_pair + part
        return y_pair

    y_tiles = [[None] * pairs_per_group for _ in range(M_GROUPS)]
    for h in range(HG_HEADS):
        gla_head(h)
        g, j = divmod(h, pairs_per_group)
        y_tiles[g][j] = ssd_pair(g, j)

    y_groups = []
    for g in range(M_GROUPS):
        b_g, c_g, x_g = group_operands(g)
        y_intra = jnp.concatenate(y_tiles[g], axis=1)
        y_inter = jnp.concatenate([_mm_nt(c_g[r], st_ssm[i, g]) for i, r in enumerate(seq_rows)], axis=0)
        y_g = y_intra + y_inter * _expand_heads(ecum, g)
        y_g = y_g + dskip_ref[:, g * M_GROUP_WIDTH:(g + 1) * M_GROUP_WIDTH] * x_g
        y_g = y_g * z_ref[:, g * M_GROUP_WIDTH:(g + 1) * M_GROUP_WIDTH]
        y_groups.append(_rms(y_g, ssmw_ref[:, g * M_GROUP_WIDTH:(g + 1) * M_GROUP_WIDTH]))

        for i, r in enumerate(seq_rows):
            cum_last = cum[r.stop - 1:r.stop, :]
            wgt = dt[r] * jnp.exp2(cum_last - cum[r])
            xw = x_g[r] * _expand_heads(wgt, g)
            decay = jnp.exp2(cum_last)
            decay_rows = jnp.concatenate(
                [jnp.broadcast_to(decay[:, hd:hd + 1], (M_HEADDIM, M_DSTATE))
                 for hd in range(g * heads_per_group, (g + 1) * heads_per_group)], axis=0)
            st_ssm[i, g] = decay_rows * st_ssm[i, g] + _mm_tn(xw, b_g[r])
    yssm_ref[...] = jnp.concatenate(y_groups, axis=1).astype(BF16)

    @pl.when(step == last_step)
    def _store_states():
        for i in range(nseq):
            for h in range(HG_HEADS):
                hg_out_ref[i, h] = st_hg[i, h]
            for g in range(M_GROUPS):
                ssm_out_ref[i, g * heads_per_group:(g + 1) * heads_per_group] = (
                    st_ssm[i, g].reshape(heads_per_group, M_HEADDIM, M_DSTATE))
            for c in range(CONV_DIM // LANES):
                conv_out_ref[i, :, c * LANES:(c + 1) * LANES] = cat_ref[i, c, 0:SUBLANES, :]


def _mixer(heads5, zs, xbc, dt, hg0, ssm0, conv0, convw, convb, alog, dskip, hgw, ssmw, prev_out, *,
           layer, batch, seq_len):
    if seq_len >= LANES:
        nseq, seq_chunk = 1, LANES
    else:
        nseq, seq_chunk = SAMPLE_SEQS_PER_STEP, seq_len
    assert seq_len % seq_chunk == 0 and batch % nseq == 0 and seq_chunk % BAND == 0
    chunk = nseq * seq_chunk
    steps = seq_len // seq_chunk
    n = batch * seq_len

    def tok_idx(b, s):
        return b * steps + s

    head_spec = pl.BlockSpec((HG_HEADS, chunk, HG_KDIM), lambda b, s: (0, tok_idx(b, s), 0))

    def tok(width):
        return pl.BlockSpec((chunk, width), lambda b, s: (tok_idx(b, s), 0))

    hg_spec = pl.BlockSpec((None, nseq, HG_HEADS, HG_KDIM, HG_VDIM), lambda b, s: (layer, b, 0, 0, 0))
    ssm_spec = pl.BlockSpec((None, nseq, M_HEADS, M_HEADDIM, M_DSTATE), lambda b, s: (layer, b, 0, 0, 0))
    conv_spec = pl.BlockSpec((None, nseq, SUBLANES, CONV_DIM), lambda b, s: (layer, b, 0, 0))
    prev_out = () if prev_out is None else tuple(prev_out)
    first_state_out = 2
    assert len(prev_out) in (0, 3)
    aliases = {_MIXER_INPUTS + j: first_state_out + j for j in range(len(prev_out))}

    return pl.pallas_call(
        functools.partial(_mixer_body, nseq, seq_chunk),
        out_shape=(
            jax.ShapeDtypeStruct((n, HG_WIDTH), BF16),
            jax.ShapeDtypeStruct((n, M_DINNER), BF16),
            jax.ShapeDtypeStruct(hg0.shape, F32),
            jax.ShapeDtypeStruct(ssm0.shape, F32),
            jax.ShapeDtypeStruct(conv0.shape, F32),
        ),
        grid=(batch // nseq, steps),
        in_specs=[head_spec] * 5 + [tok(M_DINNER), tok(CONV_DIM), tok(DT_PAD), hg_spec, ssm_spec, conv_spec,
                                    _resident((CONV_W, CONV_DIM)), _resident((1, CONV_DIM)), _resident((1, DT_PAD)),
                                    _resident((1, M_DINNER)), _resident((HG_HEADS, 1, HG_VDIM)), _resident((1, M_DINNER))]
        + [pl.BlockSpec(memory_space=pl.ANY)] * len(prev_out),
        out_specs=(tok(HG_WIDTH), tok(M_DINNER), hg_spec, ssm_spec, conv_spec),
        input_output_aliases=aliases,
        scratch_shapes=[
            pltpu.VMEM((nseq, HG_HEADS, HG_KDIM, HG_VDIM), F32),
            pltpu.VMEM((nseq, M_GROUPS, M_GROUP_WIDTH, M_DSTATE), F32),
            pltpu.VMEM((nseq, CONV_DIM // LANES, SUBLANES + seq_chunk, LANES), F32),
            pltpu.VMEM((HG_HEADS, 2, SUBLANES + chunk, HG_KDIM), F32),
        ],
        compiler_params=pltpu.CompilerParams(dimension_semantics=("arbitrary", "arbitrary"),
                                             vmem_limit_bytes=VMEM_LIMIT),
        name="mixer",
    )(*heads5, zs, xbc, dt, hg0, ssm0, conv0, convw, convb, alog, dskip, hgw, ssmw, *prev_out)


def _outproj_body(h_ref, ohg_ref, yssm_ref, g_ref, w_ref, o_ref):
    m = jnp.dot(ohg_ref[...], w_ref[0:HG_WIDTH, :], preferred_element_type=F32)
    m = m + jnp.dot(yssm_ref[...], w_ref[HG_WIDTH:HG_WIDTH + M_DINNER, :], preferred_element_type=F32)
    o_ref[...] = h_ref[...] + _rms(m, g_ref[...])


def _outproj(h, ohg, yssm, gain, w, layer):
    n = h.shape[0]
    tm = _token_tile(n)

    def tok(width):
        return pl.BlockSpec((tm, width), lambda i: (i, 0))

    return pl.pallas_call(
        _outproj_body,
        out_shape=jax.ShapeDtypeStruct((n, D_MODEL), F32),
        grid=(n // tm,),
        in_specs=[tok(D_MODEL), tok(HG_WIDTH), tok(M_DINNER), _resident((1, D_MODEL)),
                  _resident_layer(w, layer)],
        out_specs=tok(D_MODEL),
        compiler_params=pltpu.CompilerParams(dimension_semantics=("arbitrary",), vmem_limit_bytes=VMEM_LIMIT),
        name="outproj",
    )(h, ohg, yssm, gain, w)


def _prep_layer(l, w_in, hg_lb_logits, conv_w, conv_b, dt_bias, a_log, d_skip, hg_norm_w, ssm_norm_w, w_out,
                f1g, f1u, f1d, f2g, f2u, f2d, norm_gain):
    pad = DT_PAD - M_HEADS
    return dict(
        layer=l,
        w_in=w_in,
        w_dt=jnp.pad(w_in[l][:, D_PROJ_MAIN:], ((0, 0), (0, pad))),
        lb_logits=hg_lb_logits,
        conv_w=conv_w[l],
        conv_b=conv_b[l][None, :],
        dt_bias=jnp.pad(dt_bias[l], (0, pad))[None, :],
        a_log=jnp.pad(a_log[l], (0, pad))[None, :],
        d_skip=jnp.repeat(d_skip[l], M_HEADDIM)[None, :],
        hg_norm_w=hg_norm_w[l].reshape(HG_HEADS, 1, HG_VDIM),
        ssm_norm_w=ssm_norm_w[l][None, :],
        w_out=w_out.astype(BF16),
        ffn1=(f1g.astype(BF16), f1u.astype(BF16), f1d.astype(BF16)),
        ffn2=(f2g.astype(BF16), f2u.astype(BF16), f2d.astype(BF16)),
        gains=norm_gain[l],
    )


def _layer(x, hg0, ssm0, conv0, prev_out, p, *, batch, seq_len):
    gains = p["gains"]
    h = _ffn(x, gains[0:2], *p["ffn1"], p["layer"])
    *heads5, zs, xbc, dt = _inproj(h, gains[2:3], p["lb_logits"], p["dt_bias"], p["w_in"], p["w_dt"], p["layer"])
    ohg, yssm, *states = _mixer(
        heads5, zs, xbc, dt, hg0, ssm0, conv0, p["conv_w"], p["conv_b"], p["a_log"], p["d_skip"],
        p["hg_norm_w"], p["ssm_norm_w"], prev_out, layer=p["layer"], batch=batch, seq_len=seq_len)
    h = _outproj(h, ohg, yssm, gains[3:4], p["w_out"], p["layer"])
    h = _ffn(h, gains[4:6], *p["ffn2"], p["layer"])
    return h, states


def _trunk(x, s_hg, s_ssm, s_conv, layers):
    batch, seq_len, _ = x.shape
    h = x.reshape(batch * seq_len, D_MODEL)
    conv_pad = jnp.pad(s_conv, ((0, 0), (0, 0), (SUBLANES - (CONV_W - 1), 0), (0, 0)))
    states = None
    for p in layers:
        h, states = _layer(h, s_hg, s_ssm, conv_pad, states, p, batch=batch, seq_len=seq_len)
    hg_out, ssm_out, conv_out = states
    return h.reshape(batch, seq_len, D_MODEL), hg_out, ssm_out, conv_out[:, :, SUBLANES - (CONV_W - 1):, :]


def kernel(x_prompt, x_sample, state_hgrn, state_ssm, state_conv, w_in, hg_lb_logits, conv_w, conv_b, dt_bias, a_log, d_skip, hg_norm_w, ssm_norm_w, w_out, ffn1_w_gate, ffn1_w_up, ffn1_w_down, ffn2_w_gate, ffn2_w_up, ffn2_w_down, norm_gain):
    layers = [
        _prep_layer(l, w_in, hg_lb_logits, conv_w, conv_b, dt_bias, a_log, d_skip, hg_norm_w, ssm_norm_w, w_out,
                    ffn1_w_gate, ffn1_w_up, ffn1_w_down, ffn2_w_gate, ffn2_w_up, ffn2_w_down, norm_gain)
        for l in range(DEPTH)
    ]
    bp = x_prompt.shape[0]
    y_prompt, hg_p, ssm_p, conv_p = _trunk(
        x_prompt,
        jnp.zeros((DEPTH, bp, HG_HEADS, HG_KDIM, HG_VDIM), F32),
        jnp.zeros((DEPTH, bp, M_HEADS, M_HEADDIM, M_DSTATE), F32),
        jnp.zeros((DEPTH, bp, CONV_W - 1, CONV_DIM), F32),
        layers)
    y_sample, hg_s, ssm_s, conv_s = _trunk(x_sample, state_hgrn, state_ssm, state_conv, layers)
    return (y_prompt, y_sample, hg_p, ssm_p, conv_p, hg_s, ssm_s, conv_s)
```

```python
import functools

import jax
import jax.numpy as jnp
from jax import lax
from jax.experimental import pallas as pl
from jax.experimental.pallas import tpu as pltpu

F32 = jnp.float32
BF16 = jnp.bfloat16

D_MODEL = 1024
DEPTH = 2
HG_HEADS = 8
HG_KDIM = 128
HG_VDIM = 128
HG_WIDTH = HG_HEADS * HG_KDIM
LB_FLOOR = 1e-30
M_DINNER = 1024
M_HEADDIM = 64
M_HEADS = 16
M_DSTATE = 128
M_GROUPS = 2
M_GROUP_WIDTH = M_DINNER // M_GROUPS
CONV_W = 4
CONV_DIM = M_DINNER + 2 * M_GROUPS * M_DSTATE
D_FF = 2816
EPS = 1e-6

LANES = 128
SUBLANES = 8
DT_PAD = LANES
D_PROJ_MAIN = 4 * HG_WIDTH + M_DINNER + CONV_DIM
INPROJ_TILE = 512
INPROJ_COLS = 256
SAMPLE_SEQS_PER_STEP = 4
LOG2E = 1.4426950408889634
NEG_BIG = -1e30
VMEM_LIMIT = 56 * 1024 * 1024

_NT = (((1,), (1,)), ((), ()))
_TN = (((0,), (0,)), ((), ()))


def _rms(x, w):
    return x * lax.rsqrt(jnp.mean(x * x, axis=-1, keepdims=True) + EPS) * w


def _silu(x):
    return x * jax.nn.sigmoid(x)


def _mm(a, b):
    return jnp.dot(a.astype(BF16), b.astype(BF16), preferred_element_type=F32)


def _mm_nt(a, b):
    return lax.dot_general(a.astype(BF16), b.astype(BF16), _NT, preferred_element_type=F32)


def _mm_tn(a, b):
    return lax.dot_general(a.astype(BF16), b.astype(BF16), _TN, preferred_element_type=F32)


def _resident(shape):
    nd = len(shape)
    return pl.BlockSpec(shape, lambda *_: (0,) * nd, pipeline_mode=pl.Buffered(1))


def _resident_layer(stacked, layer):
    return pl.BlockSpec((None,) + stacked.shape[1:], lambda *_: (layer, 0, 0), pipeline_mode=pl.Buffered(1))


def _token_tile(n_tokens, largest=512):
    for tm in (t for t in (512, 256, 128, 64, 32, 16, 8) if t <= largest):
        if n_tokens % tm == 0:
            return tm
    raise ValueError(f"token count {n_tokens} is not a multiple of 8")


FF_CHUNK = 256


def _ffn_body(x_ref, g_ref, wg_ref, wu_ref, wd_ref, o_ref):
    x = x_ref[...]
    xn = _rms(x, g_ref[0:1, :]).astype(BF16)
    acc = None
    for j in range(D_FF // FF_CHUNK):
        sl = slice(j * FF_CHUNK, (j + 1) * FF_CHUNK)
        gate = jnp.dot(xn, wg_ref[:, sl], preferred_element_type=F32)
        up = jnp.dot(xn, wu_ref[:, sl], preferred_element_type=F32)
        act = (_silu(gate) * up).astype(BF16)
        part = jnp.dot(act, wd_ref[sl, :], preferred_element_type=F32)
        acc = part if acc is None else acc + part
    o_ref[...] = x + 0.5 * _rms(acc, g_ref[1:2, :])


def _ffn(x, gains2, wg, wu, wd, layer):
    n = x.shape[0]
    tm = _token_tile(n)
    return pl.pallas_call(
        _ffn_body,
        out_shape=jax.ShapeDtypeStruct((n, D_MODEL), F32),
        grid=(n // tm,),
        in_specs=[
            pl.BlockSpec((tm, D_MODEL), lambda i: (i, 0)),
            _resident((2, D_MODEL)),
            _resident_layer(wg, layer),
            _resident_layer(wu, layer),
            _resident_layer(wd, layer),
        ],
        out_specs=pl.BlockSpec((tm, D_MODEL), lambda i: (i, 0)),
        compiler_params=pltpu.CompilerParams(dimension_semantics=("arbitrary",), vmem_limit_bytes=VMEM_LIMIT),
        name="ffn",
    )(x, gains2, wg, wu, wd)


W_PREP_ROWS = 512


def _transpose_cast_body(wt_ref, o_ref):
    o_ref[...] = wt_ref[...].T.astype(BF16)


def _transpose_pad_body(wt_ref, o_ref):
    o_ref[...] = jnp.zeros(o_ref.shape, F32)
    o_ref[:, 0:M_HEADS] = wt_ref[...].T


def _prep_w_in(w_in):
    wt = jnp.swapaxes(w_in, 1, 2)
    w_dt = pl.pallas_call(
        _transpose_pad_body,
        out_shape=jax.ShapeDtypeStruct((DEPTH, D_MODEL, DT_PAD), F32),
        grid=(DEPTH,),
        in_specs=[pl.BlockSpec((None, M_HEADS, D_MODEL), lambda l: (l, D_PROJ_MAIN // M_HEADS, 0))],
        out_specs=pl.BlockSpec((None, D_MODEL, DT_PAD), lambda l: (l, 0, 0)),
        compiler_params=pltpu.CompilerParams(dimension_semantics=("arbitrary",)),
        name="w_dt_relayout",
    )(wt)
    w_main = pl.pallas_call(
        _transpose_cast_body,
        out_shape=jax.ShapeDtypeStruct((DEPTH, D_MODEL, D_PROJ_MAIN), BF16),
        grid=(DEPTH, D_PROJ_MAIN // W_PREP_ROWS),
        in_specs=[pl.BlockSpec((None, W_PREP_ROWS, D_MODEL), lambda l, j: (l, j, 0))],
        out_specs=pl.BlockSpec((None, D_MODEL, W_PREP_ROWS), lambda l, j: (l, 0, j)),
        compiler_params=pltpu.CompilerParams(dimension_semantics=("arbitrary", "arbitrary")),
        name="w_in_relayout",
    )(wt)
    return w_main, w_dt


_C_Q = 0
_C_F = HG_WIDTH
_C_I = 2 * HG_WIDTH
_C_G = 3 * HG_WIDTH
_C_Z = 4 * HG_WIDTH
_C_X = _C_Z + M_DINNER


def _inproj_body(layer, x_ref, g_ref, lbl_ref, dtb_ref, w_ref, wdt_ref,
                 q_ref, k_ref, gl_ref, v_ref, go_ref, z_ref, xbc_ref, dt_ref, xn_ref):
    xn_ref[...] = _rms(x_ref[...], g_ref[...]).astype(BF16)

    def proj(c0, width):
        return jnp.dot(xn_ref[...], w_ref[:, c0:c0 + width], preferred_element_type=F32)

    lg = lbl_ref[...]
    e = jnp.exp(lg - jnp.max(lg, axis=0, keepdims=True))
    p = e / jnp.sum(e, axis=0, keepdims=True)
    lb = jnp.sum(p[0:layer + 1, :], axis=0, keepdims=True) - p[0:1, :]
    lb_floor = jnp.maximum(lb, LB_FLOOR)

    def head_store(ref, c, val):
        for j in range(INPROJ_COLS // HG_KDIM):
            ref[c // HG_KDIM + j] = val[:, j * HG_KDIM:(j + 1) * HG_KDIM].astype(ref.dtype)

    for c in range(0, HG_WIDTH, INPROJ_COLS):
        cols = slice(c, c + INPROJ_COLS)
        head_store(q_ref, c, proj(_C_Q + c, INPROJ_COLS) * (HG_KDIM ** -0.5))
        fr = proj(_C_F + c, INPROJ_COLS)
        t = jnp.exp(-jnp.abs(fr))
        big = 1.0 / (1.0 + t)
        small = t * big
        positive = fr >= 0.0
        kk = (1.0 - lb[:, cols]) * jnp.where(positive, small, big)
        f_direct = lb_floor[:, cols] + (1.0 - lb[:, cols]) * jnp.where(positive, big, small)
        glog = jnp.where(positive, jnp.log1p((lb_floor[:, cols] - lb[:, cols]) - kk), jnp.log(f_direct))
        head_store(k_ref, c, kk)
        head_store(gl_ref, c, glog * LOG2E)
        head_store(v_ref, c, proj(_C_I + c, INPROJ_COLS))
        head_store(go_ref, c, _silu(proj(_C_G + c, INPROJ_COLS)))
    for c in range(0, M_DINNER, INPROJ_COLS):
        z_ref[:, c:c + INPROJ_COLS] = _silu(proj(_C_Z + c, INPROJ_COLS)).astype(BF16)
    for c in range(0, CONV_DIM, INPROJ_COLS):
        xbc_ref[:, c:c + INPROJ_COLS] = proj(_C_X + c, INPROJ_COLS)
    dt_ref[...] = jax.nn.softplus(
        jnp.dot(xn_ref[...], wdt_ref[...].astype(BF16), preferred_element_type=F32) + dtb_ref[...])


def _inproj(h, gain, lb_logits, dt_bias_pad, w, w_dt, layer):
    n = h.shape[0]
    tm = _token_tile(n, largest=INPROJ_TILE)

    def heads(dtype):
        return jax.ShapeDtypeStruct((HG_HEADS, n, HG_KDIM), dtype)

    head_spec = pl.BlockSpec((HG_HEADS, tm, HG_KDIM), lambda i: (0, i, 0))

    def tok(width):
        return pl.BlockSpec((tm, width), lambda i: (i, 0))

    return pl.pallas_call(
        functools.partial(_inproj_body, layer),
        out_shape=(heads(F32), heads(F32), heads(F32), heads(BF16), heads(BF16),
                   jax.ShapeDtypeStruct((n, M_DINNER), BF16),
                   jax.ShapeDtypeStruct((n, CONV_DIM), F32),
                   jax.ShapeDtypeStruct((n, DT_PAD), F32)),
        grid=(n // tm,),
        in_specs=[
            tok(D_MODEL),
            _resident((1, D_MODEL)),
            _resident((DEPTH, HG_WIDTH)),
            _resident((1, DT_PAD)),
            _resident_layer(w, layer),
            _resident_layer(w_dt, layer),
        ],
        out_specs=(head_spec,) * 5 + (tok(M_DINNER), tok(CONV_DIM), tok(DT_PAD)),
        scratch_shapes=[pltpu.VMEM((tm, D_MODEL), BF16)],
        compiler_params=pltpu.CompilerParams(dimension_semantics=("arbitrary",), vmem_limit_bytes=VMEM_LIMIT),
        name="inproj",
    )(h, gain, lb_logits, dt_bias_pad, w, w_dt)


BAND = 4


def _expand_heads(vals, group):
    rows = vals.shape[0]
    lane = lax.broadcasted_iota(jnp.int32, (rows, LANES), 1)
    tiles = []
    for j in range(M_GROUP_WIDTH // LANES):
        h0 = group * (M_HEADS // M_GROUPS) + 2 * j
        lo = jnp.broadcast_to(vals[:, h0:h0 + 1], (rows, LANES))
        hi = jnp.broadcast_to(vals[:, h0 + 1:h0 + 2], (rows, LANES))
        tiles.append(jnp.where(lane < M_HEADDIM, lo, hi))
    return jnp.concatenate(tiles, axis=1)


def _as_column(row_vec):
    n = row_vec.shape[1]
    hi = row_vec.astype(BF16).astype(F32)
    mid = (row_vec - hi).astype(BF16).astype(F32)
    lo = row_vec - hi - mid
    sub = lax.broadcasted_iota(jnp.int32, (SUBLANES, n), 0)
    pieces = jnp.where(sub == 0, hi, jnp.where(sub == 1, mid, jnp.where(sub == 2, lo, 0.0)))
    return _mm_tn(pieces, jnp.ones((SUBLANES, n), F32))


_MIXER_INPUTS = 17


def _mixer_body(nseq, seq_chunk, *refs):
    (q_ref, k_ref, gl_ref, v_ref, go_ref, z_ref, xbc_ref, dt_ref, hg0_ref, ssm0_ref, conv0_ref,
     convw_ref, convb_ref, alog_ref, dskip_ref, hgw_ref, ssmw_ref) = refs[:_MIXER_INPUTS]
    ohg_ref, yssm_ref, hg_out_ref, ssm_out_ref, conv_out_ref, st_hg, st_ssm, cat_ref, shift_ref = refs[-9:]
    chunk = nseq * seq_chunk
    step = pl.program_id(1)
    last_step = pl.num_programs(1) - 1
    heads_per_group = M_HEADS // M_GROUPS

    @pl.when(step == 0)
    def _load_states():
        for i in range(nseq):
            for h in range(HG_HEADS):
                st_hg[i, h] = hg0_ref[i, h]
            for g in range(M_GROUPS):
                s0 = ssm0_ref[i, g * heads_per_group:(g + 1) * heads_per_group]
                st_ssm[i, g] = s0.reshape(M_GROUP_WIDTH, M_DSTATE)
            for c in range(CONV_DIM // LANES):
                cat_ref[i, c, 0:SUBLANES, :] = conv0_ref[i, :, c * LANES:(c + 1) * LANES]
        shift_ref[:, :, 0:SUBLANES, :] = jnp.zeros((HG_HEADS, 2, SUBLANES, HG_KDIM), F32)

    row = lax.broadcasted_iota(jnp.int32, (chunk, chunk), 0)
    col = lax.broadcasted_iota(jnp.int32, (chunk, chunk), 1)
    causal = (col <= row) & ((row // seq_chunk) == (col // seq_chunk))
    causal_bf = causal.astype(F32).astype(BF16)
    seq_rows = [slice(i * seq_chunk, (i + 1) * seq_chunk) for i in range(nseq)]

    def cumsum_rows(x):
        hi = x.astype(BF16)
        rest = x - hi.astype(F32)
        mid = rest.astype(BF16)
        lo = (rest - mid.astype(F32)).astype(BF16)
        return sum(jnp.dot(causal_bf, piece, preferred_element_type=F32) for piece in (hi, mid, lo))

    levels = []
    m = BAND
    while 2 * m <= seq_chunk:
        levels.append(m)
        m *= 2
    pair_class = jnp.full((chunk, chunk), -1, jnp.int32)
    for lvl in reversed(range(len(levels))):
        blk_log2 = jnp.int32((2 * levels[lvl]).bit_length() - 1)
        same_blk = lax.shift_right_logical(row, blk_log2) == lax.shift_right_logical(col, blk_log2)
        pair_class = jnp.where(same_blk, BAND + lvl, pair_class)
    pair_class = jnp.where(row - col < BAND, row - col, pair_class)
    pair_class = jnp.where(causal, pair_class, -1)

    for h in range(HG_HEADS):
        shift_ref[h, 0, SUBLANES:SUBLANES + chunk, :] = cumsum_rows(gl_ref[h])
        shift_ref[h, 1, SUBLANES:SUBLANES + chunk, :] = k_ref[h]

    def gla_head(h):
        q = q_ref[h]
        k = k_ref[h]
        v = v_ref[h]
        b = shift_ref[h, 0, SUBLANES:SUBLANES + chunk, :]
        qe = q * jnp.exp2(b)
        o = jnp.concatenate([_mm(qe[r], st_hg[i, h]) for i, r in enumerate(seq_rows)], axis=0)

        scores = jnp.zeros((chunk, chunk), F32)
        for dist in range(BAND):
            if dist == 0:
                p = jnp.sum(q * k, axis=-1, keepdims=True)
            else:
                lo = SUBLANES - dist
                d = b - shift_ref[h, 0, lo:lo + chunk, :]
                if nseq > 1:
                    d = jnp.minimum(d, 0.0)
                p = jnp.sum(q * shift_ref[h, 1, lo:lo + chunk, :] * jnp.exp2(d), axis=-1, keepdims=True)
            scores = jnp.where(pair_class == dist, p, scores)

        for lvl, m in enumerate(levels):
            blk = 2 * m
            bref = jnp.concatenate(
                [jnp.broadcast_to(b[j * blk + m - 1:j * blk + m, :], (blk, HG_KDIM)) for j in range(chunk // blk)],
                axis=0)
            e = jnp.exp2(-jnp.abs(b - bref))
            scores = jnp.where(pair_class == BAND + lvl, _mm_nt(q * e, k * e), scores)
        o = o + _mm(scores, v)

        for i, r in enumerate(seq_rows):
            b_last = b[r.stop - 1:r.stop, :]
            kd = k[r] * jnp.exp2(b_last - b[r])
            st_hg[i, h] = _as_column(jnp.exp2(b_last)) * st_hg[i, h] + _mm_tn(kd, v[r])

        on = _rms(o, hgw_ref[h]) * go_ref[h]
        ohg_ref[:, h * HG_VDIM:(h + 1) * HG_VDIM] = on.astype(BF16)

    acts = []
    for i, r in enumerate(seq_rows):
        slabs = []
        for c in range(CONV_DIM // LANES):
            lanes = slice(c * LANES, (c + 1) * LANES)
            cat_ref[i, c, SUBLANES:SUBLANES + seq_chunk, :] = xbc_ref[r, lanes]
            conv = convb_ref[:, lanes]
            for j in range(CONV_W):
                off = SUBLANES - (CONV_W - 1) + j
                conv = conv + cat_ref[i, c, off:off + seq_chunk, :] * convw_ref[j:j + 1, lanes]
            slabs.append(_silu(conv))
            cat_ref[i, c, 0:SUBLANES, :] = cat_ref[i, c, seq_chunk:seq_chunk + SUBLANES, :]
        acts.append(jnp.concatenate(slabs, axis=1))
    act = acts[0] if nseq == 1 else jnp.concatenate(acts, axis=0)
    xs = act[:, 0:M_DINNER]

    lane_t = lax.broadcasted_iota(jnp.int32, (chunk, DT_PAD), 1)
    dt = jnp.where(lane_t < M_HEADS, dt_ref[...], 0.0)
    da = dt * (-LOG2E * jnp.exp(alog_ref[...]))
    cum = cumsum_rows(da)
    key_term = cum.T[0:M_HEADS, :] - jnp.log2(dt.T[0:M_HEADS, :])
    ecum = jnp.exp2(cum)
    lane_x = lax.broadcasted_iota(jnp.int32, (chunk, LANES), 1)
    pairs_per_group = M_GROUP_WIDTH // LANES
    assert HG_HEADS == M_GROUPS * pairs_per_group

    def group_operands(g):
        b_g = act[:, M_DINNER + g * M_DSTATE:M_DINNER + (g + 1) * M_DSTATE]
        c_g = act[:, M_DINNER + (M_GROUPS + g) * M_DSTATE:M_DINNER + (M_GROUPS + g + 1) * M_DSTATE]
        return b_g, c_g, xs[:, g * M_GROUP_WIDTH:(g + 1) * M_GROUP_WIDTH]

    cb_groups = [_mm_nt(group_operands(g)[1], group_operands(g)[0]) for g in range(M_GROUPS)]

    def ssd_pair(g, j):
        x_pair = group_operands(g)[2][:, j * LANES:(j + 1) * LANES]
        y_pair = None
        for half in range(2):
            hd = g * heads_per_group + 2 * j + half
            diff = (jnp.broadcast_to(cum[:, hd:hd + 1], (chunk, chunk))
                    - jnp.broadcast_to(key_term[hd:hd + 1, :], (chunk, chunk)))
            mh = cb_groups[g] * jnp.exp2(jnp.where(causal, diff, NEG_BIG))
            keep = (lane_x < M_HEADDIM) if half == 0 else (lane_x >= M_HEADDIM)
            part = _mm(mh, jnp.where(keep, x_pair, 0.0))
            y_pair = part if y_pair is None else y_pair + part
        return y_pair

    y_tiles = [[None] * pairs_per_group for _ in range(M_GROUPS)]
    for h in range(HG_HEADS):
        gla_head(h)
        g, j = divmod(h, pairs_per_group)
        y_tiles[g][j] = ssd_pair(g, j)

    y_groups = []
    for g in range(M_GROUPS):
        b_g, c_g, x_g = group_operands(g)
        y_intra = jnp.concatenate(y_tiles[g], axis=1)
        y_inter = jnp.concatenate([_mm_nt(c_g[r], st_ssm[i, g]) for i, r in enumerate(seq_rows)], axis=0)
        y_g = y_intra + y_inter * _expand_heads(ecum, g)
        y_g = y_g + dskip_ref[:, g * M_GROUP_WIDTH:(g + 1) * M_GROUP_WIDTH] * x_g
        y_g = y_g * z_ref[:, g * M_GROUP_WIDTH:(g + 1) * M_GROUP_WIDTH]
        y_groups.append(_rms(y_g, ssmw_ref[:, g * M_GROUP_WIDTH:(g + 1) * M_GROUP_WIDTH]))

        for i, r in enumerate(seq_rows):
            cum_last = cum[r.stop - 1:r.stop, :]
            wgt = dt[r] * jnp.exp2(cum_last - cum[r])
            xw = x_g[r] * _expand_heads(wgt, g)
            decay = jnp.exp2(cum_last)
            decay_rows = jnp.concatenate(
                [jnp.broadcast_to(decay[:, hd:hd + 1], (M_HEADDIM, M_DSTATE))
                 for hd in range(g * heads_per_group, (g + 1) * heads_per_group)], axis=0)
            st_ssm[i, g] = decay_rows * st_ssm[i, g] + _mm_tn(xw, b_g[r])
    yssm_ref[...] = jnp.concatenate(y_groups, axis=1).astype(BF16)

    @pl.when(step == last_step)
    def _store_states():
        for i in range(nseq):
            for h in range(HG_HEADS):
                hg_out_ref[i, h] = st_hg[i, h]
            for g in range(M_GROUPS):
                ssm_out_ref[i, g * heads_per_group:(g + 1) * heads_per_group] = (
                    st_ssm[i, g].reshape(heads_per_group, M_HEADDIM, M_DSTATE))
            for c in range(CONV_DIM // LANES):
                conv_out_ref[i, :, c * LANES:(c + 1) * LANES] = cat_ref[i, c, 0:SUBLANES, :]


def _mixer(heads5, zs, xbc, dt, hg0, ssm0, conv0, convw, convb, alog, dskip, hgw, ssmw, prev_out, *,
           layer, batch, seq_len):
    if seq_len >= LANES:
        nseq, seq_chunk = 1, LANES
    else:
        nseq, seq_chunk = SAMPLE_SEQS_PER_STEP, seq_len
    assert seq_len % seq_chunk == 0 and batch % nseq == 0 and seq_chunk % BAND == 0
    chunk = nseq * seq_chunk
    steps = seq_len // seq_chunk
    n = batch * seq_len

    def tok_idx(b, s):
        return b * steps + s

    head_spec = pl.BlockSpec((HG_HEADS, chunk, HG_KDIM), lambda b, s: (0, tok_idx(b, s), 0))

    def tok(width):
        return pl.BlockSpec((chunk, width), lambda b, s: (tok_idx(b, s), 0))

    hg_spec = pl.BlockSpec((None, nseq, HG_HEADS, HG_KDIM, HG_VDIM), lambda b, s: (layer, b, 0, 0, 0))
    ssm_spec = pl.BlockSpec((None, nseq, M_HEADS, M_HEADDIM, M_DSTATE), lambda b, s: (layer, b, 0, 0, 0))
    conv_spec = pl.BlockSpec((None, nseq, SUBLANES, CONV_DIM), lambda b, s: (layer, b, 0, 0))
    prev_out = () if prev_out is None else tuple(prev_out)
    first_state_out = 2
    assert len(prev_out) in (0, 3)
    aliases = {_MIXER_INPUTS + j: first_state_out + j for j in range(len(prev_out))}

    return pl.pallas_call(
        functools.partial(_mixer_body, nseq, seq_chunk),
        out_shape=(
            jax.ShapeDtypeStruct((n, HG_WIDTH), BF16),
            jax.ShapeDtypeStruct((n, M_DINNER), BF16),
            jax.ShapeDtypeStruct(hg0.shape, F32),
            jax.ShapeDtypeStruct(ssm0.shape, F32),
            jax.ShapeDtypeStruct(conv0.shape, F32),
        ),
        grid=(batch // nseq, steps),
        in_specs=[head_spec] * 5 + [tok(M_DINNER), tok(CONV_DIM), tok(DT_PAD), hg_spec, ssm_spec, conv_spec,
                                    _resident((CONV_W, CONV_DIM)), _resident((1, CONV_DIM)), _resident((1, DT_PAD)),
                                    _resident((1, M_DINNER)), _resident((HG_HEADS, 1, HG_VDIM)), _resident((1, M_DINNER))]
        + [pl.BlockSpec(memory_space=pl.ANY)] * len(prev_out),
        out_specs=(tok(HG_WIDTH), tok(M_DINNER), hg_spec, ssm_spec, conv_spec),
        input_output_aliases=aliases,
        scratch_shapes=[
            pltpu.VMEM((nseq, HG_HEADS, HG_KDIM, HG_VDIM), F32),
            pltpu.VMEM((nseq, M_GROUPS, M_GROUP_WIDTH, M_DSTATE), F32),
            pltpu.VMEM((nseq, CONV_DIM // LANES, SUBLANES + seq_chunk, LANES), F32),
            pltpu.VMEM((HG_HEADS, 2, SUBLANES + chunk, HG_KDIM), F32),
        ],
        compiler_params=pltpu.CompilerParams(dimension_semantics=("arbitrary", "arbitrary"),
                                             vmem_limit_bytes=VMEM_LIMIT),
        name="mixer",
    )(*heads5, zs, xbc, dt, hg0, ssm0, conv0, convw, convb, alog, dskip, hgw, ssmw, *prev_out)


def _outproj_body(h_ref, ohg_ref, yssm_ref, g_ref, w_ref, o_ref):
    m = jnp.dot(ohg_ref[...], w_ref[0:HG_WIDTH, :], preferred_element_type=F32)
    m = m + jnp.dot(yssm_ref[...], w_ref[HG_WIDTH:HG_WIDTH + M_DINNER, :], preferred_element_type=F32)
    o_ref[...] = h_ref[...] + _rms(m, g_ref[...])


def _outproj(h, ohg, yssm, gain, w, layer):
    n = h.shape[0]
    tm = _token_tile(n)

    def tok(width):
        return pl.BlockSpec((tm, width), lambda i: (i, 0))

    return pl.pallas_call(
        _outproj_body,
        out_shape=jax.ShapeDtypeStruct((n, D_MODEL), F32),
        grid=(n // tm,),
        in_specs=[tok(D_MODEL), tok(HG_WIDTH), tok(M_DINNER), _resident((1, D_MODEL)),
                  _resident_layer(w, layer)],
        out_specs=tok(D_MODEL),
        compiler_params=pltpu.CompilerParams(dimension_semantics=("arbitrary",), vmem_limit_bytes=VMEM_LIMIT),
        name="outproj",
    )(h, ohg, yssm, gain, w)


def _prep_layers(w_in, *rest):
    w_main, w_dt = _prep_w_in(w_in)
    return [_prep_layer(l, w_main, w_dt, *rest) for l in range(DEPTH)]


def _prep_layer(l, w_main, w_dt, hg_lb_logits, conv_w, conv_b, dt_bias, a_log, d_skip, hg_norm_w, ssm_norm_w, w_out,
                f1g, f1u, f1d, f2g, f2u, f2d, norm_gain):
    pad = DT_PAD - M_HEADS
    return dict(
        layer=l,
        w_in=w_main,
        w_dt=w_dt,
        lb_logits=hg_lb_logits,
        conv_w=conv_w[l],
        conv_b=conv_b[l][None, :],
        dt_bias=jnp.pad(dt_bias[l], (0, pad))[None, :],
        a_log=jnp.pad(a_log[l], (0, pad))[None, :],
        d_skip=jnp.repeat(d_skip[l], M_HEADDIM)[None, :],
        hg_norm_w=hg_norm_w[l].reshape(HG_HEADS, 1, HG_VDIM),
        ssm_norm_w=ssm_norm_w[l][None, :],
        w_out=w_out.astype(BF16),
        ffn1=(f1g.astype(BF16), f1u.astype(BF16), f1d.astype(BF16)),
        ffn2=(f2g.astype(BF16), f2u.astype(BF16), f2d.astype(BF16)),
        gains=norm_gain[l],
    )


def _layer(x, hg0, ssm0, conv0, prev_out, p, *, batch, seq_len):
    gains = p["gains"]
    h = _ffn(x, gains[0:2], *p["ffn1"], p["layer"])
    *heads5, zs, xbc, dt = _inproj(h, gains[2:3], p["lb_logits"], p["dt_bias"], p["w_in"], p["w_dt"], p["layer"])
    ohg, yssm, *states = _mixer(
        heads5, zs, xbc, dt, hg0, ssm0, conv0, p["conv_w"], p["conv_b"], p["a_log"], p["d_skip"],
        p["hg_norm_w"], p["ssm_norm_w"], prev_out, layer=p["layer"], batch=batch, seq_len=seq_len)
    h = _outproj(h, ohg, yssm, gains[3:4], p["w_out"], p["layer"])
    h = _ffn(h, gains[4:6], *p["ffn2"], p["layer"])
    return h, states


def _trunk(x, s_hg, s_ssm, s_conv, layers):
    batch, seq_len, _ = x.shape
    h = x.reshape(batch * seq_len, D_MODEL)
    conv_pad = jnp.pad(s_conv, ((0, 0), (0, 0), (SUBLANES - (CONV_W - 1), 0), (0, 0)))
    states = None
    for p in layers:
        h, states = _layer(h, s_hg, s_ssm, conv_pad, states, p, batch=batch, seq_len=seq_len)
    hg_out, ssm_out, conv_out = states
    return h.reshape(batch, seq_len, D_MODEL), hg_out, ssm_out, conv_out[:, :, SUBLANES - (CONV_W - 1):, :]


def kernel(x_prompt, x_sample, state_hgrn, state_ssm, state_conv, w_in, hg_lb_logits, conv_w, conv_b, dt_bias, a_log, d_skip, hg_norm_w, ssm_norm_w, w_out, ffn1_w_gate, ffn1_w_up, ffn1_w_down, ffn2_w_gate, ffn2_w_up, ffn2_w_down, norm_gain):
    layers = _prep_layers(w_in, hg_lb_logits, conv_w, conv_b, dt_bias, a_log, d_skip, hg_norm_w, ssm_norm_w, w_out,
                          ffn1_w_gate, ffn1_w_up, ffn1_w_down, ffn2_w_gate, ffn2_w_up, ffn2_w_down, norm_gain)
    bp = x_prompt.shape[0]
    y_prompt, hg_p, ssm_p, conv_p = _trunk(
        x_prompt,
        jnp.zeros((DEPTH, bp, HG_HEADS, HG_KDIM, HG_VDIM), F32),
        jnp.zeros((DEPTH, bp, M_HEADS, M_HEADDIM, M_DSTATE), F32),
        jnp.zeros((DEPTH, bp, CONV_W - 1, CONV_DIM), F32),
        layers)
    y_sample, hg_s, ssm_s, conv_s = _trunk(x_sample, state_hgrn, state_ssm, state_conv, layers)
    return (y_prompt, y_sample, hg_p, ssm_p, conv_p, hg_s, ssm_s, conv_s)
```

```python
import functools

import jax
import jax.numpy as jnp
from jax import lax
from jax.experimental import pallas as pl
from jax.experimental.pallas import tpu as pltpu

F32 = jnp.float32
BF16 = jnp.bfloat16

D_MODEL = 1024
DEPTH = 2
HG_HEADS = 8
HG_KDIM = 128
HG_VDIM = 128
HG_WIDTH = HG_HEADS * HG_KDIM
LB_FLOOR = 1e-30
M_DINNER = 1024
M_HEADDIM = 64
M_HEADS = 16
M_DSTATE = 128
M_GROUPS = 2
M_GROUP_WIDTH = M_DINNER // M_GROUPS
CONV_W = 4
CONV_DIM = M_DINNER + 2 * M_GROUPS * M_DSTATE
D_FF = 2816
EPS = 1e-6

LANES = 128
SUBLANES = 8
DT_PAD = LANES
D_PROJ_MAIN = 4 * HG_WIDTH + M_DINNER + CONV_DIM
FFN_TILE = 512
OUTPROJ_TILE = 512
INPROJ_TILE = 512
INPROJ_COLS = 256
SAMPLE_SEQS_PER_STEP = 4
LOG2E = 1.4426950408889634
NEG_BIG = -1e30
VMEM_LIMIT = 56 * 1024 * 1024

_NT = (((1,), (1,)), ((), ()))
_TN = (((0,), (0,)), ((), ()))


def _rms(x, w):
    return x * lax.rsqrt(jnp.mean(x * x, axis=-1, keepdims=True) + EPS) * w


def _silu(x):
    return x * (1.0 / (1.0 + jnp.exp2(x * (-LOG2E))))


def _mm(a, b):
    return jnp.dot(a.astype(BF16), b.astype(BF16), preferred_element_type=F32)


def _mm_nt(a, b):
    return lax.dot_general(a.astype(BF16), b.astype(BF16), _NT, preferred_element_type=F32)


def _mm_tn(a, b):
    return lax.dot_general(a.astype(BF16), b.astype(BF16), _TN, preferred_element_type=F32)


def _resident(shape):
    nd = len(shape)
    return pl.BlockSpec(shape, lambda *_: (0,) * nd, pipeline_mode=pl.Buffered(1))


def _resident_layer(stacked, layer):
    return pl.BlockSpec((None,) + stacked.shape[1:], lambda *_: (layer, 0, 0), pipeline_mode=pl.Buffered(1))


def _token_tile(n_tokens, largest):
    for tm in (t for t in (1024, 512, 256, 128, 64, 32, 16, 8) if t <= largest):
        if n_tokens % tm == 0:
            return tm
    raise ValueError(f"token count {n_tokens} is not a multiple of 8")


FF_CHUNK = 256


def _ffn_body(x_ref, g_ref, wg_ref, wu_ref, wd_ref, o_ref):
    x = x_ref[...]
    xn = _rms(x, g_ref[0:1, :]).astype(BF16)
    acc = None
    for j in range(D_FF // FF_CHUNK):
        sl = slice(j * FF_CHUNK, (j + 1) * FF_CHUNK)
        gate = jnp.dot(xn, wg_ref[:, sl], preferred_element_type=F32)
        up = jnp.dot(xn, wu_ref[:, sl], preferred_element_type=F32)
        act = (_silu(gate) * up).astype(BF16)
        part = jnp.dot(act, wd_ref[sl, :], preferred_element_type=F32)
        acc = part if acc is None else acc + part
    o_ref[...] = x + 0.5 * _rms(acc, g_ref[1:2, :])


def _ffn(x, gains2, wg, wu, wd, layer):
    n = x.shape[0]
    tm = _token_tile(n, largest=FFN_TILE)
    return pl.pallas_call(
        _ffn_body,
        out_shape=jax.ShapeDtypeStruct((n, D_MODEL), F32),
        grid=(n // tm,),
        in_specs=[
            pl.BlockSpec((tm, D_MODEL), lambda i: (i, 0)),
            _resident((2, D_MODEL)),
            _resident_layer(wg, layer),
            _resident_layer(wu, layer),
            _resident_layer(wd, layer),
        ],
        out_specs=pl.BlockSpec((tm, D_MODEL), lambda i: (i, 0)),
        compiler_params=pltpu.CompilerParams(dimension_semantics=("arbitrary",), vmem_limit_bytes=VMEM_LIMIT),
        name="ffn",
    )(x, gains2, wg, wu, wd)


W_PREP_ROWS = 512


def _transpose_cast_body(wt_ref, o_ref):
    o_ref[...] = wt_ref[...].T.astype(BF16)


def _transpose_pad_body(wt_ref, o_ref):
    o_ref[...] = jnp.zeros(o_ref.shape, F32)
    o_ref[:, 0:M_HEADS] = wt_ref[...].T


def _prep_w_in(w_in):
    wt = jnp.swapaxes(w_in, 1, 2)
    w_dt = pl.pallas_call(
        _transpose_pad_body,
        out_shape=jax.ShapeDtypeStruct((DEPTH, D_MODEL, DT_PAD), F32),
        grid=(DEPTH,),
        in_specs=[pl.BlockSpec((None, M_HEADS, D_MODEL), lambda l: (l, D_PROJ_MAIN // M_HEADS, 0))],
        out_specs=pl.BlockSpec((None, D_MODEL, DT_PAD), lambda l: (l, 0, 0)),
        compiler_params=pltpu.CompilerParams(dimension_semantics=("arbitrary",)),
        name="w_dt_relayout",
    )(wt)
    w_main = pl.pallas_call(
        _transpose_cast_body,
        out_shape=jax.ShapeDtypeStruct((DEPTH, D_MODEL, D_PROJ_MAIN), BF16),
        grid=(DEPTH, D_PROJ_MAIN // W_PREP_ROWS),
        in_specs=[pl.BlockSpec((None, W_PREP_ROWS, D_MODEL), lambda l, j: (l, j, 0))],
        out_specs=pl.BlockSpec((None, D_MODEL, W_PREP_ROWS), lambda l, j: (l, 0, j)),
        compiler_params=pltpu.CompilerParams(dimension_semantics=("arbitrary", "arbitrary")),
        name="w_in_relayout",
    )(wt)
    return w_main, w_dt


_C_Q = 0
_C_F = HG_WIDTH
_C_I = 2 * HG_WIDTH
_C_G = 3 * HG_WIDTH
_C_Z = 4 * HG_WIDTH
_C_X = _C_Z + M_DINNER


def _inproj_body(layer, x_ref, g_ref, lbl_ref, dtb_ref, w_ref, wdt_ref,
                 q_ref, k_ref, gl_ref, v_ref, go_ref, z_ref, xbc_ref, dt_ref, xn_ref):
    xn_ref[...] = _rms(x_ref[...], g_ref[...]).astype(BF16)

    def proj(c0, width):
        return jnp.dot(xn_ref[...], w_ref[:, c0:c0 + width], preferred_element_type=F32)

    lg = lbl_ref[...]
    e = jnp.exp(lg - jnp.max(lg, axis=0, keepdims=True))
    p = e / jnp.sum(e, axis=0, keepdims=True)
    lb = jnp.sum(p[0:layer + 1, :], axis=0, keepdims=True) - p[0:1, :]
    lb_floor = jnp.maximum(lb, LB_FLOOR)

    def head_store(ref, c, val):
        for j in range(INPROJ_COLS // HG_KDIM):
            ref[c // HG_KDIM + j] = val[:, j * HG_KDIM:(j + 1) * HG_KDIM].astype(ref.dtype)

    for c in range(0, HG_WIDTH, INPROJ_COLS):
        cols = slice(c, c + INPROJ_COLS)
        head_store(q_ref, c, proj(_C_Q + c, INPROJ_COLS) * (HG_KDIM ** -0.5))
        fr = proj(_C_F + c, INPROJ_COLS)
        t = jnp.exp(-jnp.abs(fr))
        big = 1.0 / (1.0 + t)
        small = t * big
        positive = fr >= 0.0
        kk = (1.0 - lb[:, cols]) * jnp.where(positive, small, big)
        f_direct = lb_floor[:, cols] + (1.0 - lb[:, cols]) * jnp.where(positive, big, small)
        glog = jnp.where(positive, jnp.log1p((lb_floor[:, cols] - lb[:, cols]) - kk), jnp.log(f_direct))
        head_store(k_ref, c, kk)
        head_store(gl_ref, c, glog * LOG2E)
        head_store(v_ref, c, proj(_C_I + c, INPROJ_COLS))
        head_store(go_ref, c, _silu(proj(_C_G + c, INPROJ_COLS)))
    for c in range(0, M_DINNER, INPROJ_COLS):
        z_ref[:, c:c + INPROJ_COLS] = _silu(proj(_C_Z + c, INPROJ_COLS)).astype(BF16)
    for c in range(0, CONV_DIM, INPROJ_COLS):
        xbc_ref[:, c:c + INPROJ_COLS] = proj(_C_X + c, INPROJ_COLS)
    dt_ref[...] = jax.nn.softplus(
        jnp.dot(xn_ref[...], wdt_ref[...].astype(BF16), preferred_element_type=F32) + dtb_ref[...])


def _inproj(h, gain, lb_logits, dt_bias_pad, w, w_dt, layer):
    n = h.shape[0]
    tm = _token_tile(n, largest=INPROJ_TILE)

    def heads(dtype):
        return jax.ShapeDtypeStruct((HG_HEADS, n, HG_KDIM), dtype)

    head_spec = pl.BlockSpec((HG_HEADS, tm, HG_KDIM), lambda i: (0, i, 0))

    def tok(width):
        return pl.BlockSpec((tm, width), lambda i: (i, 0))

    return pl.pallas_call(
        functools.partial(_inproj_body, layer),
        out_shape=(heads(F32), heads(F32), heads(F32), heads(BF16), heads(BF16),
                   jax.ShapeDtypeStruct((n, M_DINNER), BF16),
                   jax.ShapeDtypeStruct((n, CONV_DIM), F32),
                   jax.ShapeDtypeStruct((n, DT_PAD), F32)),
        grid=(n // tm,),
        in_specs=[
            tok(D_MODEL),
            _resident((1, D_MODEL)),
            _resident((DEPTH, HG_WIDTH)),
            _resident((1, DT_PAD)),
            _resident_layer(w, layer),
            _resident_layer(w_dt, layer),
        ],
        out_specs=(head_spec,) * 5 + (tok(M_DINNER), tok(CONV_DIM), tok(DT_PAD)),
        scratch_shapes=[pltpu.VMEM((tm, D_MODEL), BF16)],
        compiler_params=pltpu.CompilerParams(dimension_semantics=("arbitrary",), vmem_limit_bytes=VMEM_LIMIT),
        name="inproj",
    )(h, gain, lb_logits, dt_bias_pad, w, w_dt)


BAND_LONG = 4
BAND_SHORT = 2


def _expand_heads(vals, group):
    rows = vals.shape[0]
    lane = lax.broadcasted_iota(jnp.int32, (rows, LANES), 1)
    tiles = []
    for j in range(M_GROUP_WIDTH // LANES):
        h0 = group * (M_HEADS // M_GROUPS) + 2 * j
        lo = jnp.broadcast_to(vals[:, h0:h0 + 1], (rows, LANES))
        hi = jnp.broadcast_to(vals[:, h0 + 1:h0 + 2], (rows, LANES))
        tiles.append(jnp.where(lane < M_HEADDIM, lo, hi))
    return jnp.concatenate(tiles, axis=1)


def _as_column(row_vec):
    n = row_vec.shape[1]
    hi = row_vec.astype(BF16).astype(F32)
    mid = (row_vec - hi).astype(BF16).astype(F32)
    lo = row_vec - hi - mid
    sub = lax.broadcasted_iota(jnp.int32, (SUBLANES, n), 0)
    pieces = jnp.where(sub == 0, hi, jnp.where(sub == 1, mid, jnp.where(sub == 2, lo, 0.0)))
    return _mm_tn(pieces, jnp.ones((SUBLANES, n), F32))


_MIXER_INPUTS = 17


def _mixer_body(nseq, seq_chunk, *refs):
    band = BAND_LONG if nseq == 1 else BAND_SHORT
    (q_ref, k_ref, gl_ref, v_ref, go_ref, z_ref, xbc_ref, dt_ref, hg0_ref, ssm0_ref, conv0_ref,
     convw_ref, convb_ref, alog_ref, dskip_ref, hgw_ref, ssmw_ref) = refs[:_MIXER_INPUTS]
    (ohg_ref, yssm_ref, hg_out_ref, ssm_out_ref, conv_out_ref,
     st_hg, st_ssm, cat_ref, shift_ref, mask_ref, class_ref, act_ref) = refs[-12:]
    chunk = nseq * seq_chunk
    step = pl.program_id(1)
    last_step = pl.num_programs(1) - 1
    heads_per_group = M_HEADS // M_GROUPS

    @pl.when(step == 0)
    def _load_states():
        for i in range(nseq):
            for h in range(HG_HEADS):
                st_hg[i, h] = hg0_ref[i, h]
            for g in range(M_GROUPS):
                s0 = ssm0_ref[i, g * heads_per_group:(g + 1) * heads_per_group]
                st_ssm[i, g] = s0.reshape(M_GROUP_WIDTH, M_DSTATE)
            for c in range(CONV_DIM // LANES):
                cat_ref[i, c, 0:SUBLANES, :] = conv0_ref[i, :, c * LANES:(c + 1) * LANES]
        shift_ref[:, :, 0:SUBLANES, :] = jnp.zeros((HG_HEADS, 2, SUBLANES, HG_KDIM), F32)

    levels = []
    m = band
    while 2 * m <= seq_chunk:
        levels.append(m)
        m *= 2

    @pl.when(step == 0)
    def _build_masks():
        row = lax.broadcasted_iota(jnp.int32, (chunk, chunk), 0)
        col = lax.broadcasted_iota(jnp.int32, (chunk, chunk), 1)
        causal = (col <= row) & ((row // seq_chunk) == (col // seq_chunk))
        mask_ref[...] = causal.astype(F32).astype(BF16)
        pair_class = jnp.full((chunk, chunk), -1, jnp.int32)
        for lvl in reversed(range(len(levels))):
            blk_log2 = jnp.int32((2 * levels[lvl]).bit_length() - 1)
            same_blk = lax.shift_right_logical(row, blk_log2) == lax.shift_right_logical(col, blk_log2)
            pair_class = jnp.where(same_blk, band + lvl, pair_class)
        pair_class = jnp.where(row - col < band, row - col, pair_class)
        class_ref[...] = jnp.where(causal, pair_class, -1)

    seq_rows = [slice(i * seq_chunk, (i + 1) * seq_chunk) for i in range(nseq)]

    def cumsum_rows(x):
        hi = x.astype(BF16)
        rest = x - hi.astype(F32)
        mid = rest.astype(BF16)
        lo = (rest - mid.astype(F32)).astype(BF16)
        return sum(jnp.dot(mask_ref[...], piece, preferred_element_type=F32) for piece in (hi, mid, lo))

    for h in range(HG_HEADS):
        shift_ref[h, 0, SUBLANES:SUBLANES + chunk, :] = cumsum_rows(gl_ref[h])
        shift_ref[h, 1, SUBLANES:SUBLANES + chunk, :] = k_ref[h]

    def gla_head(h):
        q = q_ref[h]
        k = k_ref[h]
        v = v_ref[h]
        b = shift_ref[h, 0, SUBLANES:SUBLANES + chunk, :]
        qe = q * jnp.exp2(b)
        o = jnp.concatenate([_mm(qe[r], st_hg[i, h]) for i, r in enumerate(seq_rows)], axis=0)

        scores = jnp.zeros((chunk, chunk), F32)
        for dist in range(band):
            if dist == 0:
                p = jnp.sum(q * k, axis=-1, keepdims=True)
            else:
                lo = SUBLANES - dist
                d = b - shift_ref[h, 0, lo:lo + chunk, :]
                if nseq > 1:
                    d = jnp.minimum(d, 0.0)
                p = jnp.sum(q * shift_ref[h, 1, lo:lo + chunk, :] * jnp.exp2(d), axis=-1, keepdims=True)
            scores = jnp.where(class_ref[...] == dist, p, scores)

        for lvl, m in enumerate(levels):
            blk = 2 * m
            bref = jnp.concatenate(
                [jnp.broadcast_to(b[j * blk + m - 1:j * blk + m, :], (blk, HG_KDIM)) for j in range(chunk // blk)],
                axis=0)
            e = jnp.exp2(-jnp.abs(b - bref))
            scores = jnp.where(class_ref[...] == band + lvl, _mm_nt(q * e, k * e), scores)
        o = o + _mm(scores, v)

        for i, r in enumerate(seq_rows):
            b_last = b[r.stop - 1:r.stop, :]
            kd = k[r] * jnp.exp2(b_last - b[r])
            st_hg[i, h] = _as_column(jnp.exp2(b_last)) * st_hg[i, h] + _mm_tn(kd, v[r])

        on = _rms(o, hgw_ref[h]) * go_ref[h]
        ohg_ref[:, h * HG_VDIM:(h + 1) * HG_VDIM] = on.astype(BF16)

    for i, r in enumerate(seq_rows):
        for c in range(CONV_DIM // LANES):
            lanes = slice(c * LANES, (c + 1) * LANES)
            cat_ref[i, c, SUBLANES:SUBLANES + seq_chunk, :] = xbc_ref[r, lanes]
            conv = convb_ref[:, lanes]
            for j in range(CONV_W):
                off = SUBLANES - (CONV_W - 1) + j
                conv = conv + cat_ref[i, c, off:off + seq_chunk, :] * convw_ref[j:j + 1, lanes]
            act_ref[r, lanes] = _silu(conv)
            cat_ref[i, c, 0:SUBLANES, :] = cat_ref[i, c, seq_chunk:seq_chunk + SUBLANES, :]

    lane_t = lax.broadcasted_iota(jnp.int32, (chunk, DT_PAD), 1)
    dt = jnp.where(lane_t < M_HEADS, dt_ref[...], 0.0)
    da = dt * (-LOG2E * jnp.exp(alog_ref[...]))
    cum = cumsum_rows(da)
    key_term = cum.T[0:M_HEADS, :] - jnp.log2(dt.T[0:M_HEADS, :])
    ecum = jnp.exp2(cum)
    lane_x = lax.broadcasted_iota(jnp.int32, (chunk, LANES), 1)
    pairs_per_group = M_GROUP_WIDTH // LANES
    assert HG_HEADS == M_GROUPS * pairs_per_group

    def group_operands(g):
        b_g = act_ref[:, M_DINNER + g * M_DSTATE:M_DINNER + (g + 1) * M_DSTATE]
        c_g = act_ref[:, M_DINNER + (M_GROUPS + g) * M_DSTATE:M_DINNER + (M_GROUPS + g + 1) * M_DSTATE]
        return b_g, c_g, act_ref[:, g * M_GROUP_WIDTH:(g + 1) * M_GROUP_WIDTH]

    cb_groups = [_mm_nt(group_operands(g)[1], group_operands(g)[0]) for g in range(M_GROUPS)]

    def ssd_pair(g, j):
        x_pair = act_ref[:, g * M_GROUP_WIDTH + j * LANES:g * M_GROUP_WIDTH + (j + 1) * LANES]
        causal = class_ref[...] >= 0
        y_pair = None
        for half in range(2):
            hd = g * heads_per_group + 2 * j + half
            diff = (jnp.broadcast_to(cum[:, hd:hd + 1], (chunk, chunk))
                    - jnp.broadcast_to(key_term[hd:hd + 1, :], (chunk, chunk)))
            mh = cb_groups[g] * jnp.exp2(jnp.where(causal, diff, NEG_BIG))
            keep = (lane_x < M_HEADDIM) if half == 0 else (lane_x >= M_HEADDIM)
            part = _mm(mh, jnp.where(keep, x_pair, 0.0))
            y_pair = part if y_pair is None else y_pair + part
        return y_pair

    y_tiles = [[None] * pairs_per_group for _ in range(M_GROUPS)]
    for h in range(HG_HEADS):
        gla_head(h)
        g, j = divmod(h, pairs_per_group)
        y_tiles[g][j] = ssd_pair(g, j)

    y_groups = []
    for g in range(M_GROUPS):
        b_g, c_g, x_g = group_operands(g)
        y_intra = jnp.concatenate(y_tiles[g], axis=1)
        y_inter = jnp.concatenate([_mm_nt(c_g[r], st_ssm[i, g]) for i, r in enumerate(seq_rows)], axis=0)
        y_g = y_intra + y_inter * _expand_heads(ecum, g)
        y_g = y_g + dskip_ref[:, g * M_GROUP_WIDTH:(g + 1) * M_GROUP_WIDTH] * x_g
        y_g = y_g * z_ref[:, g * M_GROUP_WIDTH:(g + 1) * M_GROUP_WIDTH]
        y_groups.append(_rms(y_g, ssmw_ref[:, g * M_GROUP_WIDTH:(g + 1) * M_GROUP_WIDTH]))

        for i, r in enumerate(seq_rows):
            cum_last = cum[r.stop - 1:r.stop, :]
            wgt = dt[r] * jnp.exp2(cum_last - cum[r])
            xw = x_g[r] * _expand_heads(wgt, g)
            decay = jnp.exp2(cum_last)
            decay_rows = jnp.concatenate(
                [jnp.broadcast_to(decay[:, hd:hd + 1], (M_HEADDIM, M_DSTATE))
                 for hd in range(g * heads_per_group, (g + 1) * heads_per_group)], axis=0)
            st_ssm[i, g] = decay_rows * st_ssm[i, g] + _mm_tn(xw, b_g[r])
    yssm_ref[...] = jnp.concatenate(y_groups, axis=1).astype(BF16)

    @pl.when(step == last_step)
    def _store_states():
        for i in range(nseq):
            for h in range(HG_HEADS):
                hg_out_ref[i, h] = st_hg[i, h]
            for g in range(M_GROUPS):
                ssm_out_ref[i, g * heads_per_group:(g + 1) * heads_per_group] = (
                    st_ssm[i, g].reshape(heads_per_group, M_HEADDIM, M_DSTATE))
            for c in range(CONV_DIM // LANES):
                conv_out_ref[i, :, c * LANES:(c + 1) * LANES] = cat_ref[i, c, 0:SUBLANES, :]


def _mixer(heads5, zs, xbc, dt, hg0, ssm0, conv0, convw, convb, alog, dskip, hgw, ssmw, prev_out, *,
           layer, batch, seq_len):
    if seq_len >= LANES:
        nseq, seq_chunk = 1, LANES
    else:
        nseq, seq_chunk = SAMPLE_SEQS_PER_STEP, seq_len
    assert seq_len % seq_chunk == 0 and batch % nseq == 0 and seq_chunk % max(BAND_LONG, BAND_SHORT) == 0
    chunk = nseq * seq_chunk
    steps = seq_len // seq_chunk
    n = batch * seq_len

    def tok_idx(b, s):
        return b * steps + s

    head_spec = pl.BlockSpec((HG_HEADS, chunk, HG_KDIM), lambda b, s: (0, tok_idx(b, s), 0))

    def tok(width):
        return pl.BlockSpec((chunk, width), lambda b, s: (tok_idx(b, s), 0))

    hg_spec = pl.BlockSpec((None, nseq, HG_HEADS, HG_KDIM, HG_VDIM), lambda b, s: (layer, b, 0, 0, 0))
    ssm_spec = pl.BlockSpec((None, nseq, M_HEADS, M_HEADDIM, M_DSTATE), lambda b, s: (layer, b, 0, 0, 0))
    conv_spec = pl.BlockSpec((None, nseq, SUBLANES, CONV_DIM), lambda b, s: (layer, b, 0, 0))
    prev_out = () if prev_out is None else tuple(prev_out)
    first_state_out = 2
    assert len(prev_out) in (0, 3)
    aliases = {_MIXER_INPUTS + j: first_state_out + j for j in range(len(prev_out))}

    return pl.pallas_call(
        functools.partial(_mixer_body, nseq, seq_chunk),
        out_shape=(
            jax.ShapeDtypeStruct((n, HG_WIDTH), BF16),
            jax.ShapeDtypeStruct((n, M_DINNER), BF16),
            jax.ShapeDtypeStruct(hg0.shape, F32),
            jax.ShapeDtypeStruct(ssm0.shape, F32),
            jax.ShapeDtypeStruct(conv0.shape, F32),
        ),
        grid=(batch // nseq, steps),
        in_specs=[head_spec] * 5 + [tok(M_DINNER), tok(CONV_DIM), tok(DT_PAD), hg_spec, ssm_spec, conv_spec,
                                    _resident((CONV_W, CONV_DIM)), _resident((1, CONV_DIM)), _resident((1, DT_PAD)),
                                    _resident((1, M_DINNER)), _resident((HG_HEADS, 1, HG_VDIM)), _resident((1, M_DINNER))]
        + [pl.BlockSpec(memory_space=pl.ANY)] * len(prev_out),
        out_specs=(tok(HG_WIDTH), tok(M_DINNER), hg_spec, ssm_spec, conv_spec),
        input_output_aliases=aliases,
        scratch_shapes=[
            pltpu.VMEM((nseq, HG_HEADS, HG_KDIM, HG_VDIM), F32),
            pltpu.VMEM((nseq, M_GROUPS, M_GROUP_WIDTH, M_DSTATE), F32),
            pltpu.VMEM((nseq, CONV_DIM // LANES, SUBLANES + seq_chunk, LANES), F32),
            pltpu.VMEM((HG_HEADS, 2, SUBLANES + chunk, HG_KDIM), F32),
            pltpu.VMEM((chunk, chunk), BF16),
            pltpu.VMEM((chunk, chunk), jnp.int32),
            pltpu.VMEM((chunk, CONV_DIM), F32),
        ],
        compiler_params=pltpu.CompilerParams(dimension_semantics=("arbitrary", "arbitrary"),
                                             vmem_limit_bytes=VMEM_LIMIT),
        name="mixer",
    )(*heads5, zs, xbc, dt, hg0, ssm0, conv0, convw, convb, alog, dskip, hgw, ssmw, *prev_out)


def _outproj_body(h_ref, ohg_ref, yssm_ref, g_ref, w_ref, o_ref):
    m = jnp.dot(ohg_ref[...], w_ref[0:HG_WIDTH, :], preferred_element_type=F32)
    m = m + jnp.dot(yssm_ref[...], w_ref[HG_WIDTH:HG_WIDTH + M_DINNER, :], preferred_element_type=F32)
    o_ref[...] = h_ref[...] + _rms(m, g_ref[...])


def _outproj(h, ohg, yssm, gain, w, layer):
    n = h.shape[0]
    tm = _token_tile(n, largest=OUTPROJ_TILE)

    def tok(width):
        return pl.BlockSpec((tm, width), lambda i: (i, 0))

    return pl.pallas_call(
        _outproj_body,
        out_shape=jax.ShapeDtypeStruct((n, D_MODEL), F32),
        grid=(n // tm,),
        in_specs=[tok(D_MODEL), tok(HG_WIDTH), tok(M_DINNER), _resident((1, D_MODEL)),
                  _resident_layer(w, layer)],
        out_specs=tok(D_MODEL),
        compiler_params=pltpu.CompilerParams(dimension_semantics=("arbitrary",), vmem_limit_bytes=VMEM_LIMIT),
        name="outproj",
    )(h, ohg, yssm, gain, w)


def _prep_layers(w_in, *rest):
    w_main, w_dt = _prep_w_in(w_in)
    return [_prep_layer(l, w_main, w_dt, *rest) for l in range(DEPTH)]


def _prep_layer(l, w_main, w_dt, hg_lb_logits, conv_w, conv_b, dt_bias, a_log, d_skip, hg_norm_w, ssm_norm_w, w_out,
                f1g, f1u, f1d, f2g, f2u, f2d, norm_gain):
    pad = DT_PAD - M_HEADS
    return dict(
        layer=l,
        w_in=w_main,
        w_dt=w_dt,
        lb_logits=hg_lb_logits,
        conv_w=conv_w[l],
        conv_b=conv_b[l][None, :],
        dt_bias=jnp.pad(dt_bias[l], (0, pad))[None, :],
        a_log=jnp.pad(a_log[l], (0, pad))[None, :],
        d_skip=jnp.repeat(d_skip[l], M_HEADDIM)[None, :],
        hg_norm_w=hg_norm_w[l].reshape(HG_HEADS, 1, HG_VDIM),
        ssm_norm_w=ssm_norm_w[l][None, :],
        w_out=w_out.astype(BF16),
        ffn1=(f1g.astype(BF16), f1u.astype(BF16), f1d.astype(BF16)),
        ffn2=(f2g.astype(BF16), f2u.astype(BF16), f2d.astype(BF16)),
        gains=norm_gain[l],
    )


def _layer(x, hg0, ssm0, conv0, prev_out, p, *, batch, seq_len):
    gains = p["gains"]
    h = _ffn(x, gains[0:2], *p["ffn1"], p["layer"])
    *heads5, zs, xbc, dt = _inproj(h, gains[2:3], p["lb_logits"], p["dt_bias"], p["w_in"], p["w_dt"], p["layer"])
    ohg, yssm, *states = _mixer(
        heads5, zs, xbc, dt, hg0, ssm0, conv0, p["conv_w"], p["conv_b"], p["a_log"], p["d_skip"],
        p["hg_norm_w"], p["ssm_norm_w"], prev_out, layer=p["layer"], batch=batch, seq_len=seq_len)
    h = _outproj(h, ohg, yssm, gains[3:4], p["w_out"], p["layer"])
    h = _ffn(h, gains[4:6], *p["ffn2"], p["layer"])
    return h, states


def _trunk(x, s_hg, s_ssm, s_conv, layers):
    batch, seq_len, _ = x.shape
    h = x.reshape(batch * seq_len, D_MODEL)
    conv_pad = jnp.pad(s_conv, ((0, 0), (0, 0), (SUBLANES - (CONV_W - 1), 0), (0, 0)))
    states = None
    for p in layers:
        h, states = _layer(h, s_hg, s_ssm, conv_pad, states, p, batch=batch, seq_len=seq_len)
    hg_out, ssm_out, conv_out = states
    return h.reshape(batch, seq_len, D_MODEL), hg_out, ssm_out, conv_out[:, :, SUBLANES - (CONV_W - 1):, :]


def kernel(x_prompt, x_sample, state_hgrn, state_ssm, state_conv, w_in, hg_lb_logits, conv_w, conv_b, dt_bias, a_log, d_skip, hg_norm_w, ssm_norm_w, w_out, ffn1_w_gate, ffn1_w_up, ffn1_w_down, ffn2_w_gate, ffn2_w_up, ffn2_w_down, norm_gain):
    layers = _prep_layers(w_in, hg_lb_logits, conv_w, conv_b, dt_bias, a_log, d_skip, hg_norm_w, ssm_norm_w, w_out,
                          ffn1_w_gate, ffn1_w_up, ffn1_w_down, ffn2_w_gate, ffn2_w_up, ffn2_w_down, norm_gain)
    bp = x_prompt.shape[0]
    y_prompt, hg_p, ssm_p, conv_p = _trunk(
        x_prompt,
        jnp.zeros((DEPTH, bp, HG_HEADS, HG_KDIM, HG_VDIM), F32),
        jnp.zeros((DEPTH, bp, M_HEADS, M_HEADDIM, M_DSTATE), F32),
        jnp.zeros((DEPTH, bp, CONV_W - 1, CONV_DIM), F32),
        layers)
    y_sample, hg_s, ssm_s, conv_s = _trunk(x_sample, state_hgrn, state_ssm, state_conv, layers)
    return (y_prompt, y_sample, hg_p, ssm_p, conv_p, hg_s, ssm_s, conv_s)
```

```python
import functools

import jax
import jax.numpy as jnp
from jax import lax
from jax.experimental import pallas as pl
from jax.experimental.pallas import tpu as pltpu

F32 = jnp.float32
BF16 = jnp.bfloat16

D_MODEL = 1024
DEPTH = 2
HG_HEADS = 8
HG_KDIM = 128
HG_VDIM = 128
HG_WIDTH = HG_HEADS * HG_KDIM
LB_FLOOR = 1e-30
M_DINNER = 1024
M_HEADDIM = 64
M_HEADS = 16
M_DSTATE = 128
M_GROUPS = 2
M_GROUP_WIDTH = M_DINNER // M_GROUPS
CONV_W = 4
CONV_DIM = M_DINNER + 2 * M_GROUPS * M_DSTATE
D_FF = 2816
EPS = 1e-6

LANES = 128
SUBLANES = 8
DT_PAD = LANES
D_PROJ_MAIN = 4 * HG_WIDTH + M_DINNER + CONV_DIM
FFN_TILE = 512
OUTPROJ_TILE = 512
INPROJ_TILE = 512
INPROJ_COLS = 256
SAMPLE_SEQS_PER_STEP = 4
LOG2E = 1.4426950408889634
NEG_BIG = -1e30
VMEM_LIMIT = 56 * 1024 * 1024

_NT = (((1,), (1,)), ((), ()))
_TN = (((0,), (0,)), ((), ()))


def _rms(x, w):
    return x * lax.rsqrt(jnp.mean(x * x, axis=-1, keepdims=True) + EPS) * w


def _silu(x):
    return x * (1.0 / (1.0 + jnp.exp2(x * (-LOG2E))))


def _mm(a, b):
    return jnp.dot(a.astype(BF16), b.astype(BF16), preferred_element_type=F32)


def _mm_nt(a, b):
    return lax.dot_general(a.astype(BF16), b.astype(BF16), _NT, preferred_element_type=F32)


def _mm_tn(a, b):
    return lax.dot_general(a.astype(BF16), b.astype(BF16), _TN, preferred_element_type=F32)


def _resident(shape):
    nd = len(shape)
    return pl.BlockSpec(shape, lambda *_: (0,) * nd, pipeline_mode=pl.Buffered(1))


def _resident_layer(stacked, layer):
    return pl.BlockSpec((None,) + stacked.shape[1:], lambda *_: (layer, 0, 0), pipeline_mode=pl.Buffered(1))


def _token_tile(n_tokens, largest):
    for tm in (t for t in (1024, 512, 256, 128, 64, 32, 16, 8) if t <= largest):
        if n_tokens % tm == 0:
            return tm
    raise ValueError(f"token count {n_tokens} is not a multiple of 8")


FF_CHUNK = 256


def _ffn_body(x_ref, g_ref, wg_ref, wu_ref, wd_ref, o_ref):
    x = x_ref[...]
    xn = _rms(x, g_ref[0:1, :]).astype(BF16)
    acc = None
    for j in range(D_FF // FF_CHUNK):
        sl = slice(j * FF_CHUNK, (j + 1) * FF_CHUNK)
        gate = jnp.dot(xn, wg_ref[:, sl], preferred_element_type=F32)
        up = jnp.dot(xn, wu_ref[:, sl], preferred_element_type=F32)
        act = (_silu(gate) * up).astype(BF16)
        part = jnp.dot(act, wd_ref[sl, :], preferred_element_type=F32)
        acc = part if acc is None else acc + part
    o_ref[...] = x + 0.5 * _rms(acc, g_ref[1:2, :])


def _ffn(x, gains2, wg, wu, wd, layer):
    n = x.shape[0]
    tm = _token_tile(n, largest=FFN_TILE)
    return pl.pallas_call(
        _ffn_body,
        out_shape=jax.ShapeDtypeStruct((n, D_MODEL), F32),
        grid=(n // tm,),
        in_specs=[
            pl.BlockSpec((tm, D_MODEL), lambda i: (i, 0)),
            _resident((2, D_MODEL)),
            _resident_layer(wg, layer),
            _resident_layer(wu, layer),
            _resident_layer(wd, layer),
        ],
        out_specs=pl.BlockSpec((tm, D_MODEL), lambda i: (i, 0)),
        compiler_params=pltpu.CompilerParams(dimension_semantics=("arbitrary",), vmem_limit_bytes=VMEM_LIMIT),
        name="ffn",
    )(x, gains2, wg, wu, wd)


W_PREP_ROWS = 512


def _transpose_cast_body(wt_ref, o_ref):
    o_ref[...] = wt_ref[...].T.astype(BF16)


def _transpose_pad_body(wt_ref, o_ref):
    o_ref[...] = jnp.zeros(o_ref.shape, F32)
    o_ref[:, 0:M_HEADS] = wt_ref[...].T


def _prep_w_in(w_in):
    wt = jnp.swapaxes(w_in, 1, 2)
    w_dt = pl.pallas_call(
        _transpose_pad_body,
        out_shape=jax.ShapeDtypeStruct((DEPTH, D_MODEL, DT_PAD), F32),
        grid=(DEPTH,),
        in_specs=[pl.BlockSpec((None, M_HEADS, D_MODEL), lambda l: (l, D_PROJ_MAIN // M_HEADS, 0))],
        out_specs=pl.BlockSpec((None, D_MODEL, DT_PAD), lambda l: (l, 0, 0)),
        compiler_params=pltpu.CompilerParams(dimension_semantics=("arbitrary",)),
        name="w_dt_relayout",
    )(wt)
    w_main = pl.pallas_call(
        _transpose_cast_body,
        out_shape=jax.ShapeDtypeStruct((DEPTH, D_MODEL, D_PROJ_MAIN), BF16),
        grid=(DEPTH, D_PROJ_MAIN // W_PREP_ROWS),
        in_specs=[pl.BlockSpec((None, W_PREP_ROWS, D_MODEL), lambda l, j: (l, j, 0))],
        out_specs=pl.BlockSpec((None, D_MODEL, W_PREP_ROWS), lambda l, j: (l, 0, j)),
        compiler_params=pltpu.CompilerParams(dimension_semantics=("arbitrary", "arbitrary")),
        name="w_in_relayout",
    )(wt)
    return w_main, w_dt


_C_Q = 0
_C_F = HG_WIDTH
_C_I = 2 * HG_WIDTH
_C_G = 3 * HG_WIDTH
_C_Z = 4 * HG_WIDTH
_C_X = _C_Z + M_DINNER


def _inproj_body(layer, x_ref, g_ref, lbl_ref, dtb_ref, w_ref, wdt_ref,
                 q_ref, k_ref, gl_ref, v_ref, go_ref, z_ref, xbc_ref, dt_ref, xn_ref):
    xn_ref[...] = _rms(x_ref[...], g_ref[...]).astype(BF16)

    def proj(c0, width):
        return jnp.dot(xn_ref[...], w_ref[:, c0:c0 + width], preferred_element_type=F32)

    lg = lbl_ref[...]
    e = jnp.exp(lg - jnp.max(lg, axis=0, keepdims=True))
    p = e / jnp.sum(e, axis=0, keepdims=True)
    lb = jnp.sum(p[0:layer + 1, :], axis=0, keepdims=True) - p[0:1, :]
    lb_floor = jnp.maximum(lb, LB_FLOOR)

    def head_store(ref, c, val):
        for j in range(INPROJ_COLS // HG_KDIM):
            ref[c // HG_KDIM + j] = val[:, j * HG_KDIM:(j + 1) * HG_KDIM].astype(ref.dtype)

    for c in range(0, HG_WIDTH, INPROJ_COLS):
        cols = slice(c, c + INPROJ_COLS)
        head_store(q_ref, c, proj(_C_Q + c, INPROJ_COLS) * (HG_KDIM ** -0.5))
        fr = proj(_C_F + c, INPROJ_COLS)
        t = jnp.exp(-jnp.abs(fr))
        big = 1.0 / (1.0 + t)
        small = t * big
        positive = fr >= 0.0
        kk = (1.0 - lb[:, cols]) * jnp.where(positive, small, big)
        f_direct = lb_floor[:, cols] + (1.0 - lb[:, cols]) * jnp.where(positive, big, small)
        glog = jnp.where(positive, jnp.log1p((lb_floor[:, cols] - lb[:, cols]) - kk), jnp.log(f_direct))
        head_store(k_ref, c, kk)
        head_store(gl_ref, c, glog * LOG2E)
        head_store(v_ref, c, proj(_C_I + c, INPROJ_COLS))
        head_store(go_ref, c, _silu(proj(_C_G + c, INPROJ_COLS)))
    for c in range(0, M_DINNER, INPROJ_COLS):
        z_ref[:, c:c + INPROJ_COLS] = _silu(proj(_C_Z + c, INPROJ_COLS)).astype(BF16)
    for c in range(0, CONV_DIM, INPROJ_COLS):
        xbc_ref[:, c:c + INPROJ_COLS] = proj(_C_X + c, INPROJ_COLS)
    dt_ref[...] = jax.nn.softplus(
        jnp.dot(xn_ref[...], wdt_ref[...].astype(BF16), preferred_element_type=F32) + dtb_ref[...])


def _inproj(h, gain, lb_logits, dt_bias_pad, w, w_dt, layer):
    n = h.shape[0]
    tm = _token_tile(n, largest=INPROJ_TILE)

    def heads(dtype):
        return jax.ShapeDtypeStruct((HG_HEADS, n, HG_KDIM), dtype)

    head_spec = pl.BlockSpec((HG_HEADS, tm, HG_KDIM), lambda i: (0, i, 0))

    def tok(width):
        return pl.BlockSpec((tm, width), lambda i: (i, 0))

    return pl.pallas_call(
        functools.partial(_inproj_body, layer),
        out_shape=(heads(F32), heads(F32), heads(F32), heads(BF16), heads(BF16),
                   jax.ShapeDtypeStruct((n, M_DINNER), BF16),
                   jax.ShapeDtypeStruct((n, CONV_DIM), F32),
                   jax.ShapeDtypeStruct((n, DT_PAD), F32)),
        grid=(n // tm,),
        in_specs=[
            tok(D_MODEL),
            _resident((1, D_MODEL)),
            _resident((DEPTH, HG_WIDTH)),
            _resident((1, DT_PAD)),
            _resident_layer(w, layer),
            _resident_layer(w_dt, layer),
        ],
        out_specs=(head_spec,) * 5 + (tok(M_DINNER), tok(CONV_DIM), tok(DT_PAD)),
        scratch_shapes=[pltpu.VMEM((tm, D_MODEL), BF16)],
        compiler_params=pltpu.CompilerParams(dimension_semantics=("arbitrary",), vmem_limit_bytes=VMEM_LIMIT),
        name="inproj",
    )(h, gain, lb_logits, dt_bias_pad, w, w_dt)


BAND_LONG = 4
BAND_SHORT = 2


def _expand_heads(vals, group):
    rows = vals.shape[0]
    lane = lax.broadcasted_iota(jnp.int32, (rows, LANES), 1)
    tiles = []
    for j in range(M_GROUP_WIDTH // LANES):
        h0 = group * (M_HEADS // M_GROUPS) + 2 * j
        lo = jnp.broadcast_to(vals[:, h0:h0 + 1], (rows, LANES))
        hi = jnp.broadcast_to(vals[:, h0 + 1:h0 + 2], (rows, LANES))
        tiles.append(jnp.where(lane < M_HEADDIM, lo, hi))
    return jnp.concatenate(tiles, axis=1)


def _as_column(row_vec):
    n = row_vec.shape[1]
    hi = row_vec.astype(BF16).astype(F32)
    mid = (row_vec - hi).astype(BF16).astype(F32)
    lo = row_vec - hi - mid
    sub = lax.broadcasted_iota(jnp.int32, (SUBLANES, n), 0)
    pieces = jnp.where(sub == 0, hi, jnp.where(sub == 1, mid, jnp.where(sub == 2, lo, 0.0)))
    return _mm_tn(pieces, jnp.ones((SUBLANES, n), F32))


_MIXER_INPUTS = 17


def _mixer_body(nseq, seq_chunk, steps, *refs):
    band = BAND_LONG if nseq == 1 else BAND_SHORT
    (q_ref, k_ref, gl_ref, v_ref, go_ref, z_ref, xbc_ref, dt_ref, hg0_ref, ssm0_ref, conv0_ref,
     convw_ref, convb_ref, alog_ref, dskip_ref, hgw_ref, ssmw_ref) = refs[:_MIXER_INPUTS]
    (ohg_ref, yssm_ref, hg_out_ref, ssm_out_ref, conv_out_ref,
     st_hg, st_ssm, cat_ref, shift_ref, mask_ref, class_ref, act_ref) = refs[-12:]
    chunk = nseq * seq_chunk
    step = pl.program_id(1)
    last_step = steps - 1
    heads_per_group = M_HEADS // M_GROUPS
    carried = steps > 1

    def when_step(which):
        if carried:
            return pl.when(step == which)
        return lambda block: block()

    def group_heads(g):
        return slice(g * heads_per_group, (g + 1) * heads_per_group)

    def read_hg(i, h):
        return st_hg[i, h] if carried else hg0_ref[i, h]

    def write_hg(i, h, val):
        if carried:
            st_hg[i, h] = val
        else:
            hg_out_ref[i, h] = val

    def read_ssm(i, g):
        return st_ssm[i, g] if carried else ssm0_ref[i, group_heads(g)].reshape(M_GROUP_WIDTH, M_DSTATE)

    def write_ssm(i, g, val):
        if carried:
            st_ssm[i, g] = val
        else:
            ssm_out_ref[i, group_heads(g)] = val.reshape(heads_per_group, M_HEADDIM, M_DSTATE)

    @when_step(0)
    def _load_states():
        for i in range(nseq):
            if carried:
                for h in range(HG_HEADS):
                    st_hg[i, h] = hg0_ref[i, h]
                for g in range(M_GROUPS):
                    st_ssm[i, g] = ssm0_ref[i, group_heads(g)].reshape(M_GROUP_WIDTH, M_DSTATE)
            for c in range(CONV_DIM // LANES):
                cat_ref[i, c, 0:SUBLANES, :] = conv0_ref[i, :, c * LANES:(c + 1) * LANES]
        shift_ref[:, :, 0:SUBLANES, :] = jnp.zeros((HG_HEADS, 2, SUBLANES, HG_KDIM), F32)

    levels = []
    m = band
    while 2 * m <= seq_chunk:
        levels.append(m)
        m *= 2

    @when_step(0)
    def _build_masks():
        row = lax.broadcasted_iota(jnp.int32, (chunk, chunk), 0)
        col = lax.broadcasted_iota(jnp.int32, (chunk, chunk), 1)
        causal = (col <= row) & ((row // seq_chunk) == (col // seq_chunk))
        mask_ref[...] = causal.astype(F32).astype(BF16)
        pair_class = jnp.full((chunk, chunk), -1, jnp.int32)
        for lvl in reversed(range(len(levels))):
            blk_log2 = jnp.int32((2 * levels[lvl]).bit_length() - 1)
            same_blk = lax.shift_right_logical(row, blk_log2) == lax.shift_right_logical(col, blk_log2)
            pair_class = jnp.where(same_blk, band + lvl, pair_class)
        pair_class = jnp.where(row - col < band, row - col, pair_class)
        class_ref[...] = jnp.where(causal, pair_class, -1)

    seq_rows = [slice(i * seq_chunk, (i + 1) * seq_chunk) for i in range(nseq)]

    def cumsum_rows(x):
        hi = x.astype(BF16)
        rest = x - hi.astype(F32)
        mid = rest.astype(BF16)
        lo = (rest - mid.astype(F32)).astype(BF16)
        return sum(jnp.dot(mask_ref[...], piece, preferred_element_type=F32) for piece in (hi, mid, lo))

    for h in range(HG_HEADS):
        shift_ref[h, 0, SUBLANES:SUBLANES + chunk, :] = cumsum_rows(gl_ref[h])
        shift_ref[h, 1, SUBLANES:SUBLANES + chunk, :] = k_ref[h]

    def gla_head(h):
        q = q_ref[h]
        k = k_ref[h]
        v = v_ref[h]
        b = shift_ref[h, 0, SUBLANES:SUBLANES + chunk, :]
        qe = q * jnp.exp2(b)
        o = jnp.concatenate([_mm(qe[r], read_hg(i, h)) for i, r in enumerate(seq_rows)], axis=0)

        scores = jnp.zeros((chunk, chunk), F32)
        for dist in range(band):
            if dist == 0:
                p = jnp.sum(q * k, axis=-1, keepdims=True)
            else:
                lo = SUBLANES - dist
                d = b - shift_ref[h, 0, lo:lo + chunk, :]
                if nseq > 1:
                    d = jnp.minimum(d, 0.0)
                p = jnp.sum(q * shift_ref[h, 1, lo:lo + chunk, :] * jnp.exp2(d), axis=-1, keepdims=True)
            scores = jnp.where(class_ref[...] == dist, p, scores)

        for lvl, m in enumerate(levels):
            blk = 2 * m
            bref = jnp.concatenate(
                [jnp.broadcast_to(b[j * blk + m - 1:j * blk + m, :], (blk, HG_KDIM)) for j in range(chunk // blk)],
                axis=0)
            e = jnp.exp2(-jnp.abs(b - bref))
            scores = jnp.where(class_ref[...] == band + lvl, _mm_nt(q * e, k * e), scores)
        o = o + _mm(scores, v)

        for i, r in enumerate(seq_rows):
            b_last = b[r.stop - 1:r.stop, :]
            kd = k[r] * jnp.exp2(b_last - b[r])
            write_hg(i, h, _as_column(jnp.exp2(b_last)) * read_hg(i, h) + _mm_tn(kd, v[r]))

        on = _rms(o, hgw_ref[h]) * go_ref[h]
        ohg_ref[:, h * HG_VDIM:(h + 1) * HG_VDIM] = on.astype(BF16)

    for i, r in enumerate(seq_rows):
        for c in range(CONV_DIM // LANES):
            lanes = slice(c * LANES, (c + 1) * LANES)
            cat_ref[i, c, SUBLANES:SUBLANES + seq_chunk, :] = xbc_ref[r, lanes]
            conv = convb_ref[:, lanes]
            for j in range(CONV_W):
                off = SUBLANES - (CONV_W - 1) + j
                conv = conv + cat_ref[i, c, off:off + seq_chunk, :] * convw_ref[j:j + 1, lanes]
            act_ref[r, lanes] = _silu(conv)
            cat_ref[i, c, 0:SUBLANES, :] = cat_ref[i, c, seq_chunk:seq_chunk + SUBLANES, :]

    lane_t = lax.broadcasted_iota(jnp.int32, (chunk, DT_PAD), 1)
    dt = jnp.where(lane_t < M_HEADS, dt_ref[...], 0.0)
    da = dt * (-LOG2E * jnp.exp(alog_ref[...]))
    cum = cumsum_rows(da)
    key_term = cum.T[0:M_HEADS, :] - jnp.log2(dt.T[0:M_HEADS, :])
    ecum = jnp.exp2(cum)
    lane_x = lax.broadcasted_iota(jnp.int32, (chunk, LANES), 1)
    pairs_per_group = M_GROUP_WIDTH // LANES
    assert HG_HEADS == M_GROUPS * pairs_per_group

    def group_operands(g):
        b_g = act_ref[:, M_DINNER + g * M_DSTATE:M_DINNER + (g + 1) * M_DSTATE]
        c_g = act_ref[:, M_DINNER + (M_GROUPS + g) * M_DSTATE:M_DINNER + (M_GROUPS + g + 1) * M_DSTATE]
        return b_g, c_g, act_ref[:, g * M_GROUP_WIDTH:(g + 1) * M_GROUP_WIDTH]

    cb_groups = [_mm_nt(group_operands(g)[1], group_operands(g)[0]) for g in range(M_GROUPS)]

    def ssd_pair(g, j):
        x_pair = act_ref[:, g * M_GROUP_WIDTH + j * LANES:g * M_GROUP_WIDTH + (j + 1) * LANES]
        causal = class_ref[...] >= 0
        y_pair = None
        for half in range(2):
            hd = g * heads_per_group + 2 * j + half
            diff = (jnp.broadcast_to(cum[:, hd:hd + 1], (chunk, chunk))
                    - jnp.broadcast_to(key_term[hd:hd + 1, :], (chunk, chunk)))
            mh = cb_groups[g] * jnp.exp2(jnp.where(causal, diff, NEG_BIG))
            keep = (lane_x < M_HEADDIM) if half == 0 else (lane_x >= M_HEADDIM)
            part = _mm(mh, jnp.where(keep, x_pair, 0.0))
            y_pair = part if y_pair is None else y_pair + part
        return y_pair

    y_tiles = [[None] * pairs_per_group for _ in range(M_GROUPS)]
    for h in range(HG_HEADS):
        gla_head(h)
        g, j = divmod(h, pairs_per_group)
        y_tiles[g][j] = ssd_pair(g, j)

    y_groups = []
    for g in range(M_GROUPS):
        b_g, c_g, x_g = group_operands(g)
        y_intra = jnp.concatenate(y_tiles[g], axis=1)
        y_inter = jnp.concatenate([_mm_nt(c_g[r], read_ssm(i, g)) for i, r in enumerate(seq_rows)], axis=0)
        y_g = y_intra + y_inter * _expand_heads(ecum, g)
        y_g = y_g + dskip_ref[:, g * M_GROUP_WIDTH:(g + 1) * M_GROUP_WIDTH] * x_g
        y_g = y_g * z_ref[:, g * M_GROUP_WIDTH:(g + 1) * M_GROUP_WIDTH]
        y_groups.append(_rms(y_g, ssmw_ref[:, g * M_GROUP_WIDTH:(g + 1) * M_GROUP_WIDTH]))

        for i, r in enumerate(seq_rows):
            cum_last = cum[r.stop - 1:r.stop, :]
            wgt = dt[r] * jnp.exp2(cum_last - cum[r])
            xw = x_g[r] * _expand_heads(wgt, g)
            decay = jnp.exp2(cum_last)
            decay_rows = jnp.concatenate(
                [jnp.broadcast_to(decay[:, hd:hd + 1], (M_HEADDIM, M_DSTATE))
                 for hd in range(g * heads_per_group, (g + 1) * heads_per_group)], axis=0)
            write_ssm(i, g, decay_rows * read_ssm(i, g) + _mm_tn(xw, b_g[r]))
    yssm_ref[...] = jnp.concatenate(y_groups, axis=1).astype(BF16)

    @when_step(last_step)
    def _store_states():
        for i in range(nseq):
            if carried:
                for h in range(HG_HEADS):
                    hg_out_ref[i, h] = st_hg[i, h]
                for g in range(M_GROUPS):
                    ssm_out_ref[i, group_heads(g)] = st_ssm[i, g].reshape(heads_per_group, M_HEADDIM, M_DSTATE)
            for c in range(CONV_DIM // LANES):
                conv_out_ref[i, :, c * LANES:(c + 1) * LANES] = cat_ref[i, c, 0:SUBLANES, :]


def _mixer(heads5, zs, xbc, dt, hg0, ssm0, conv0, convw, convb, alog, dskip, hgw, ssmw, prev_out, *,
           layer, batch, seq_len):
    if seq_len >= LANES:
        nseq, seq_chunk = 1, LANES
    else:
        nseq, seq_chunk = SAMPLE_SEQS_PER_STEP, seq_len
    assert seq_len % seq_chunk == 0 and batch % nseq == 0 and seq_chunk % max(BAND_LONG, BAND_SHORT) == 0
    chunk = nseq * seq_chunk
    steps = seq_len // seq_chunk
    n = batch * seq_len

    def tok_idx(b, s):
        return b * steps + s

    head_spec = pl.BlockSpec((HG_HEADS, chunk, HG_KDIM), lambda b, s: (0, tok_idx(b, s), 0))

    def tok(width):
        return pl.BlockSpec((chunk, width), lambda b, s: (tok_idx(b, s), 0))

    hg_spec = pl.BlockSpec((None, nseq, HG_HEADS, HG_KDIM, HG_VDIM), lambda b, s: (layer, b, 0, 0, 0))
    ssm_spec = pl.BlockSpec((None, nseq, M_HEADS, M_HEADDIM, M_DSTATE), lambda b, s: (layer, b, 0, 0, 0))
    conv_spec = pl.BlockSpec((None, nseq, SUBLANES, CONV_DIM), lambda b, s: (layer, b, 0, 0))
    prev_out = () if prev_out is None else tuple(prev_out)
    first_state_out = 2
    assert len(prev_out) in (0, 3)
    aliases = {_MIXER_INPUTS + j: first_state_out + j for j in range(len(prev_out))}

    return pl.pallas_call(
        functools.partial(_mixer_body, nseq, seq_chunk, steps),
        out_shape=(
            jax.ShapeDtypeStruct((n, HG_WIDTH), BF16),
            jax.ShapeDtypeStruct((n, M_DINNER), BF16),
            jax.ShapeDtypeStruct(hg0.shape, F32),
            jax.ShapeDtypeStruct(ssm0.shape, F32),
            jax.ShapeDtypeStruct(conv0.shape, F32),
        ),
        grid=(batch // nseq, steps),
        in_specs=[head_spec] * 5 + [tok(M_DINNER), tok(CONV_DIM), tok(DT_PAD), hg_spec, ssm_spec, conv_spec,
                                    _resident((CONV_W, CONV_DIM)), _resident((1, CONV_DIM)), _resident((1, DT_PAD)),
                                    _resident((1, M_DINNER)), _resident((HG_HEADS, 1, HG_VDIM)), _resident((1, M_DINNER))]
        + [pl.BlockSpec(memory_space=pl.ANY)] * len(prev_out),
        out_specs=(tok(HG_WIDTH), tok(M_DINNER), hg_spec, ssm_spec, conv_spec),
        input_output_aliases=aliases,
        scratch_shapes=[
            pltpu.VMEM((nseq if steps > 1 else 0, HG_HEADS, HG_KDIM, HG_VDIM), F32),
            pltpu.VMEM((nseq if steps > 1 else 0, M_GROUPS, M_GROUP_WIDTH, M_DSTATE), F32),
            pltpu.VMEM((nseq, CONV_DIM // LANES, SUBLANES + seq_chunk, LANES), F32),
            pltpu.VMEM((HG_HEADS, 2, SUBLANES + chunk, HG_KDIM), F32),
            pltpu.VMEM((chunk, chunk), BF16),
            pltpu.VMEM((chunk, chunk), jnp.int32),
            pltpu.VMEM((chunk, CONV_DIM), F32),
        ],
        compiler_params=pltpu.CompilerParams(dimension_semantics=("arbitrary", "arbitrary"),
                                             vmem_limit_bytes=VMEM_LIMIT),
        name="mixer",
    )(*heads5, zs, xbc, dt, hg0, ssm0, conv0, convw, convb, alog, dskip, hgw, ssmw, *prev_out)


def _outproj_body(h_ref, ohg_ref, yssm_ref, g_ref, w_ref, o_ref):
    m = jnp.dot(ohg_ref[...], w_ref[0:HG_WIDTH, :], preferred_element_type=F32)
    m = m + jnp.dot(yssm_ref[...], w_ref[HG_WIDTH:HG_WIDTH + M_DINNER, :], preferred_element_type=F32)
    o_ref[...] = h_ref[...] + _rms(m, g_ref[...])


def _outproj(h, ohg, yssm, gain, w, layer):
    n = h.shape[0]
    tm = _token_tile(n, largest=OUTPROJ_TILE)

    def tok(width):
        return pl.BlockSpec((tm, width), lambda i: (i, 0))

    return pl.pallas_call(
        _outproj_body,
        out_shape=jax.ShapeDtypeStruct((n, D_MODEL), F32),
        grid=(n // tm,),
        in_specs=[tok(D_MODEL), tok(HG_WIDTH), tok(M_DINNER), _resident((1, D_MODEL)),
                  _resident_layer(w, layer)],
        out_specs=tok(D_MODEL),
        compiler_params=pltpu.CompilerParams(dimension_semantics=("arbitrary",), vmem_limit_bytes=VMEM_LIMIT),
        name="outproj",
    )(h, ohg, yssm, gain, w)


def _prep_layers(w_in, *rest):
    w_main, w_dt = _prep_w_in(w_in)
    return [_prep_layer(l, w_main, w_dt, *rest) for l in range(DEPTH)]


def _prep_layer(l, w_main, w_dt, hg_lb_logits, conv_w, conv_b, dt_bias, a_log, d_skip, hg_norm_w, ssm_norm_w, w_out,
                f1g, f1u, f1d, f2g, f2u, f2d, norm_gain):
    pad = DT_PAD - M_HEADS
    return dict(
        layer=l,
        w_in=w_main,
        w_dt=w_dt,
        lb_logits=hg_lb_logits,
        conv_w=conv_w[l],
        conv_b=conv_b[l][None, :],
        dt_bias=jnp.pad(dt_bias[l], (0, pad))[None, :],
        a_log=jnp.pad(a_log[l], (0, pad))[None, :],
        d_skip=jnp.repeat(d_skip[l], M_HEADDIM)[None, :],
        hg_norm_w=hg_norm_w[l].reshape(HG_HEADS, 1, HG_VDIM),
        ssm_norm_w=ssm_norm_w[l][None, :],
        w_out=w_out.astype(BF16),
        ffn1=(f1g.astype(BF16), f1u.astype(BF16), f1d.astype(BF16)),
        ffn2=(f2g.astype(BF16), f2u.astype(BF16), f2d.astype(BF16)),
        gains=norm_gain[l],
    )


def _layer(x, hg0, ssm0, conv0, prev_out, p, *, batch, seq_len):
    gains = p["gains"]
    h = _ffn(x, gains[0:2], *p["ffn1"], p["layer"])
    *heads5, zs, xbc, dt = _inproj(h, gains[2:3], p["lb_logits"], p["dt_bias"], p["w_in"], p["w_dt"], p["layer"])
    ohg, yssm, *states = _mixer(
        heads5, zs, xbc, dt, hg0, ssm0, conv0, p["conv_w"], p["conv_b"], p["a_log"], p["d_skip"],
        p["hg_norm_w"], p["ssm_norm_w"], prev_out, layer=p["layer"], batch=batch, seq_len=seq_len)
    h = _outproj(h, ohg, yssm, gains[3:4], p["w_out"], p["layer"])
    h = _ffn(h, gains[4:6], *p["ffn2"], p["layer"])
    return h, states


def _trunk(x, s_hg, s_ssm, s_conv, layers):
    batch, seq_len, _ = x.shape
    h = x.reshape(batch * seq_len, D_MODEL)
    conv_pad = jnp.pad(s_conv, ((0, 0), (0, 0), (SUBLANES - (CONV_W - 1), 0), (0, 0)))
    states = None
    for p in layers:
        h, states = _layer(h, s_hg, s_ssm, conv_pad, states, p, batch=batch, seq_len=seq_len)
    hg_out, ssm_out, conv_out = states
    return h.reshape(batch, seq_len, D_MODEL), hg_out, ssm_out, conv_out[:, :, SUBLANES - (CONV_W - 1):, :]


def kernel(x_prompt, x_sample, state_hgrn, state_ssm, state_conv, w_in, hg_lb_logits, conv_w, conv_b, dt_bias, a_log, d_skip, hg_norm_w, ssm_norm_w, w_out, ffn1_w_gate, ffn1_w_up, ffn1_w_down, ffn2_w_gate, ffn2_w_up, ffn2_w_down, norm_gain):
    layers = _prep_layers(w_in, hg_lb_logits, conv_w, conv_b, dt_bias, a_log, d_skip, hg_norm_w, ssm_norm_w, w_out,
                          ffn1_w_gate, ffn1_w_up, ffn1_w_down, ffn2_w_gate, ffn2_w_up, ffn2_w_down, norm_gain)
    bp = x_prompt.shape[0]
    y_prompt, hg_p, ssm_p, conv_p = _trunk(
        x_prompt,
        jnp.zeros((DEPTH, bp, HG_HEADS, HG_KDIM, HG_VDIM), F32),
        jnp.zeros((DEPTH, bp, M_HEADS, M_HEADDIM, M_DSTATE), F32),
        jnp.zeros((DEPTH, bp, CONV_W - 1, CONV_DIM), F32),
        layers)
    y_sample, hg_s, ssm_s, conv_s = _trunk(x_sample, state_hgrn, state_ssm, state_conv, layers)
    return (y_prompt, y_sample, hg_p, ssm_p, conv_p, hg_s, ssm_s, conv_s)
```

```python
import functools

import jax
import jax.numpy as jnp
from jax import lax
from jax.experimental import pallas as pl
from jax.experimental.pallas import tpu as pltpu

F32 = jnp.float32
BF16 = jnp.bfloat16

D_MODEL = 1024
DEPTH = 2
HG_HEADS = 8
HG_KDIM = 128
HG_VDIM = 128
HG_WIDTH = HG_HEADS * HG_KDIM
LB_FLOOR = 1e-30
M_DINNER = 1024
M_HEADDIM = 64
M_HEADS = 16
M_DSTATE = 128
M_GROUPS = 2
M_GROUP_WIDTH = M_DINNER // M_GROUPS
CONV_W = 4
CONV_DIM = M_DINNER + 2 * M_GROUPS * M_DSTATE
D_FF = 2816
EPS = 1e-6

LANES = 128
SUBLANES = 8
DT_PAD = LANES
D_PROJ_MAIN = 4 * HG_WIDTH + M_DINNER + CONV_DIM
FFN_TILE = 512
OUTPROJ_TILE = 512
INPROJ_TILE = 512
INPROJ_COLS = 256
SAMPLE_SEQS_PER_STEP = 4
LOG2E = 1.4426950408889634
NEG_BIG = -1e30
VMEM_LIMIT = 56 * 1024 * 1024

_NT = (((1,), (1,)), ((), ()))
_TN = (((0,), (0,)), ((), ()))


def _rms(x, w):
    return x * lax.rsqrt(jnp.mean(x * x, axis=-1, keepdims=True) + EPS) * w


def _silu(x):
    return x * (1.0 / (1.0 + jnp.exp2(x * (-LOG2E))))


def _mm(a, b):
    return jnp.dot(a.astype(BF16), b.astype(BF16), preferred_element_type=F32)


def _mm_nt(a, b):
    return lax.dot_general(a.astype(BF16), b.astype(BF16), _NT, preferred_element_type=F32)


def _mm_tn(a, b):
    return lax.dot_general(a.astype(BF16), b.astype(BF16), _TN, preferred_element_type=F32)


def _resident(shape):
    nd = len(shape)
    return pl.BlockSpec(shape, lambda *_: (0,) * nd, pipeline_mode=pl.Buffered(1))


def _resident_layer(stacked, layer):
    return pl.BlockSpec((None,) + stacked.shape[1:], lambda *_: (layer, 0, 0), pipeline_mode=pl.Buffered(1))


def _token_tile(n_tokens, largest):
    for tm in (t for t in (1024, 512, 256, 128, 64, 32, 16, 8) if t <= largest):
        if n_tokens % tm == 0:
            return tm
    raise ValueError(f"token count {n_tokens} is not a multiple of 8")


FF_CHUNK = 256


def _ffn_body(x_ref, g_ref, wg_ref, wu_ref, wd_ref, o_ref, act_ref):
    x = x_ref[...]
    xn = _rms(x, g_ref[0:1, :]).astype(BF16)
    for j in range(D_FF // FF_CHUNK):
        sl = slice(j * FF_CHUNK, (j + 1) * FF_CHUNK)
        gate = jnp.dot(xn, wg_ref[:, sl], preferred_element_type=F32)
        up = jnp.dot(xn, wu_ref[:, sl], preferred_element_type=F32)
        act_ref[:, sl] = (_silu(gate) * up).astype(BF16)
    down = jnp.dot(act_ref[...], wd_ref[...], preferred_element_type=F32)
    o_ref[...] = x + 0.5 * _rms(down, g_ref[1:2, :])


def _ffn(x, gains2, wg, wu, wd, layer):
    n = x.shape[0]
    tm = _token_tile(n, largest=FFN_TILE)
    return pl.pallas_call(
        _ffn_body,
        out_shape=jax.ShapeDtypeStruct((n, D_MODEL), F32),
        grid=(n // tm,),
        in_specs=[
            pl.BlockSpec((tm, D_MODEL), lambda i: (i, 0)),
            _resident((2, D_MODEL)),
            _resident_layer(wg, layer),
            _resident_layer(wu, layer),
            _resident_layer(wd, layer),
        ],
        out_specs=pl.BlockSpec((tm, D_MODEL), lambda i: (i, 0)),
        scratch_shapes=[pltpu.VMEM((tm, D_FF), BF16)],
        compiler_params=pltpu.CompilerParams(dimension_semantics=("arbitrary",), vmem_limit_bytes=VMEM_LIMIT),
        name="ffn",
    )(x, gains2, wg, wu, wd)


W_PREP_ROWS = 512


def _transpose_cast_body(wt_ref, o_ref):
    o_ref[...] = wt_ref[...].T.astype(BF16)


def _transpose_pad_body(wt_ref, o_ref):
    o_ref[...] = jnp.zeros(o_ref.shape, F32)
    o_ref[:, 0:M_HEADS] = wt_ref[...].T


def _prep_w_in(w_in):
    wt = jnp.swapaxes(w_in, 1, 2)
    w_dt = pl.pallas_call(
        _transpose_pad_body,
        out_shape=jax.ShapeDtypeStruct((DEPTH, D_MODEL, DT_PAD), F32),
        grid=(DEPTH,),
        in_specs=[pl.BlockSpec((None, M_HEADS, D_MODEL), lambda l: (l, D_PROJ_MAIN // M_HEADS, 0))],
        out_specs=pl.BlockSpec((None, D_MODEL, DT_PAD), lambda l: (l, 0, 0)),
        compiler_params=pltpu.CompilerParams(dimension_semantics=("arbitrary",)),
        name="w_dt_relayout",
    )(wt)
    w_main = pl.pallas_call(
        _transpose_cast_body,
        out_shape=jax.ShapeDtypeStruct((DEPTH, D_MODEL, D_PROJ_MAIN), BF16),
        grid=(DEPTH, D_PROJ_MAIN // W_PREP_ROWS),
        in_specs=[pl.BlockSpec((None, W_PREP_ROWS, D_MODEL), lambda l, j: (l, j, 0))],
        out_specs=pl.BlockSpec((None, D_MODEL, W_PREP_ROWS), lambda l, j: (l, 0, j)),
        compiler_params=pltpu.CompilerParams(dimension_semantics=("arbitrary", "arbitrary")),
        name="w_in_relayout",
    )(wt)
    return w_main, w_dt


_C_Q = 0
_C_F = HG_WIDTH
_C_I = 2 * HG_WIDTH
_C_G = 3 * HG_WIDTH
_C_Z = 4 * HG_WIDTH
_C_X = _C_Z + M_DINNER


def _inproj_body(layer, x_ref, g_ref, lbl_ref, dtb_ref, w_ref, wdt_ref,
                 q_ref, k_ref, gl_ref, v_ref, go_ref, z_ref, xbc_ref, dt_ref, xn_ref):
    xn_ref[...] = _rms(x_ref[...], g_ref[...]).astype(BF16)

    def proj(c0, width):
        return jnp.dot(xn_ref[...], w_ref[:, c0:c0 + width], preferred_element_type=F32)

    lg = lbl_ref[...]
    e = jnp.exp(lg - jnp.max(lg, axis=0, keepdims=True))
    p = e / jnp.sum(e, axis=0, keepdims=True)
    lb = jnp.sum(p[0:layer + 1, :], axis=0, keepdims=True) - p[0:1, :]
    lb_floor = jnp.maximum(lb, LB_FLOOR)

    def head_store(ref, c, val):
        for j in range(INPROJ_COLS // HG_KDIM):
            ref[c // HG_KDIM + j] = val[:, j * HG_KDIM:(j + 1) * HG_KDIM].astype(ref.dtype)

    def gate_block(c):
        cols = slice(c, c + INPROJ_COLS)
        fr = proj(_C_F + c, INPROJ_COLS)
        t = jnp.exp(-jnp.abs(fr))
        big = 1.0 / (1.0 + t)
        small = t * big
        positive = fr >= 0.0
        kk = (1.0 - lb[:, cols]) * jnp.where(positive, small, big)
        f_direct = lb_floor[:, cols] + (1.0 - lb[:, cols]) * jnp.where(positive, big, small)
        glog = jnp.where(positive, jnp.log1p((lb_floor[:, cols] - lb[:, cols]) - kk), jnp.log(f_direct))
        head_store(k_ref, c, kk)
        head_store(gl_ref, c, glog * LOG2E)

    def xbc_block(c):
        xbc_ref[:, c:c + INPROJ_COLS] = proj(_C_X + c, INPROJ_COLS)

    xbc_blocks = list(range(0, CONV_DIM, INPROJ_COLS))
    for c in range(0, HG_WIDTH, INPROJ_COLS):
        head_store(q_ref, c, proj(_C_Q + c, INPROJ_COLS) * (HG_KDIM ** -0.5))
        gate_block(c)
        xbc_block(xbc_blocks.pop())
        head_store(v_ref, c, proj(_C_I + c, INPROJ_COLS))
        head_store(go_ref, c, _silu(proj(_C_G + c, INPROJ_COLS)))
        z_ref[:, c:c + INPROJ_COLS] = _silu(proj(_C_Z + c, INPROJ_COLS)).astype(BF16)
    for c in xbc_blocks:
        xbc_block(c)
    dt_ref[...] = jax.nn.softplus(
        jnp.dot(xn_ref[...], wdt_ref[...].astype(BF16), preferred_element_type=F32) + dtb_ref[...])


def _inproj(h, gain, lb_logits, dt_bias_pad, w, w_dt, layer):
    n = h.shape[0]
    tm = _token_tile(n, largest=INPROJ_TILE)

    def heads(dtype):
        return jax.ShapeDtypeStruct((HG_HEADS, n, HG_KDIM), dtype)

    head_spec = pl.BlockSpec((HG_HEADS, tm, HG_KDIM), lambda i: (0, i, 0))

    def tok(width):
        return pl.BlockSpec((tm, width), lambda i: (i, 0))

    return pl.pallas_call(
        functools.partial(_inproj_body, layer),
        out_shape=(heads(F32), heads(F32), heads(F32), heads(BF16), heads(BF16),
                   jax.ShapeDtypeStruct((n, M_DINNER), BF16),
                   jax.ShapeDtypeStruct((n, CONV_DIM), F32),
                   jax.ShapeDtypeStruct((n, DT_PAD), F32)),
        grid=(n // tm,),
        in_specs=[
            tok(D_MODEL),
            _resident((1, D_MODEL)),
            _resident((DEPTH, HG_WIDTH)),
            _resident((1, DT_PAD)),
            _resident_layer(w, layer),
            _resident_layer(w_dt, layer),
        ],
        out_specs=(head_spec,) * 5 + (tok(M_DINNER), tok(CONV_DIM), tok(DT_PAD)),
        scratch_shapes=[pltpu.VMEM((tm, D_MODEL), BF16)],
        compiler_params=pltpu.CompilerParams(dimension_semantics=("arbitrary",), vmem_limit_bytes=VMEM_LIMIT),
        name="inproj",
    )(h, gain, lb_logits, dt_bias_pad, w, w_dt)


BAND_LONG = 4
BAND_SHORT = 2


def _expand_heads(vals, group):
    rows = vals.shape[0]
    lane = lax.broadcasted_iota(jnp.int32, (rows, LANES), 1)
    tiles = []
    for j in range(M_GROUP_WIDTH // LANES):
        h0 = group * (M_HEADS // M_GROUPS) + 2 * j
        lo = jnp.broadcast_to(vals[:, h0:h0 + 1], (rows, LANES))
        hi = jnp.broadcast_to(vals[:, h0 + 1:h0 + 2], (rows, LANES))
        tiles.append(jnp.where(lane < M_HEADDIM, lo, hi))
    return jnp.concatenate(tiles, axis=1)


def _as_column(row_vec):
    n = row_vec.shape[1]
    hi = row_vec.astype(BF16).astype(F32)
    mid = (row_vec - hi).astype(BF16).astype(F32)
    lo = row_vec - hi - mid
    sub = lax.broadcasted_iota(jnp.int32, (SUBLANES, n), 0)
    pieces = jnp.where(sub == 0, hi, jnp.where(sub == 1, mid, jnp.where(sub == 2, lo, 0.0)))
    return _mm_tn(pieces, jnp.ones((SUBLANES, n), F32))


_MIXER_INPUTS = 17


def _mixer_body(nseq, seq_chunk, steps, *refs):
    band = BAND_LONG if nseq == 1 else BAND_SHORT
    (q_ref, k_ref, gl_ref, v_ref, go_ref, z_ref, xbc_ref, dt_ref, hg0_ref, ssm0_ref, conv0_ref,
     convw_ref, convb_ref, alog_ref, dskip_ref, hgw_ref, ssmw_ref) = refs[:_MIXER_INPUTS]
    (ohg_ref, yssm_ref, hg_out_ref, ssm_out_ref, conv_out_ref,
     st_hg, st_ssm, cat_ref, shift_ref, mask_ref, class_ref, act_ref) = refs[-12:]
    chunk = nseq * seq_chunk
    step = pl.program_id(1)
    last_step = steps - 1
    heads_per_group = M_HEADS // M_GROUPS
    carried = steps > 1

    def when_step(which):
        if carried:
            return pl.when(step == which)
        return lambda block: block()

    def group_heads(g):
        return slice(g * heads_per_group, (g + 1) * heads_per_group)

    def read_hg(i, h):
        return st_hg[i, h] if carried else hg0_ref[i, h]

    def write_hg(i, h, val):
        if carried:
            st_hg[i, h] = val
        else:
            hg_out_ref[i, h] = val

    def read_ssm(i, g):
        return st_ssm[i, g] if carried else ssm0_ref[i, group_heads(g)].reshape(M_GROUP_WIDTH, M_DSTATE)

    def write_ssm(i, g, val):
        if carried:
            st_ssm[i, g] = val
        else:
            ssm_out_ref[i, group_heads(g)] = val.reshape(heads_per_group, M_HEADDIM, M_DSTATE)

    @when_step(0)
    def _load_states():
        for i in range(nseq):
            if carried:
                for h in range(HG_HEADS):
                    st_hg[i, h] = hg0_ref[i, h]
                for g in range(M_GROUPS):
                    st_ssm[i, g] = ssm0_ref[i, group_heads(g)].reshape(M_GROUP_WIDTH, M_DSTATE)
            for c in range(CONV_DIM // LANES):
                cat_ref[i, c, 0:SUBLANES, :] = conv0_ref[i, :, c * LANES:(c + 1) * LANES]
        shift_ref[:, :, 0:SUBLANES, :] = jnp.zeros((HG_HEADS, 2, SUBLANES, HG_KDIM), F32)

    levels = []
    m = band
    while 2 * m <= seq_chunk:
        levels.append(m)
        m *= 2

    @when_step(0)
    def _build_masks():
        row = lax.broadcasted_iota(jnp.int32, (chunk, chunk), 0)
        col = lax.broadcasted_iota(jnp.int32, (chunk, chunk), 1)
        causal = (col <= row) & ((row // seq_chunk) == (col // seq_chunk))
        mask_ref[...] = causal.astype(F32).astype(BF16)
        pair_class = jnp.full((chunk, chunk), -1, jnp.int32)
        for lvl in reversed(range(len(levels))):
            blk_log2 = jnp.int32((2 * levels[lvl]).bit_length() - 1)
            same_blk = lax.shift_right_logical(row, blk_log2) == lax.shift_right_logical(col, blk_log2)
            pair_class = jnp.where(same_blk, band + lvl, pair_class)
        pair_class = jnp.where(row - col < band, row - col, pair_class)
        class_ref[...] = jnp.where(causal, pair_class, -1)

    seq_rows = [slice(i * seq_chunk, (i + 1) * seq_chunk) for i in range(nseq)]

    def cumsum_rows(x):
        hi = x.astype(BF16)
        rest = x - hi.astype(F32)
        mid = rest.astype(BF16)
        lo = (rest - mid.astype(F32)).astype(BF16)
        return sum(jnp.dot(mask_ref[...], piece, preferred_element_type=F32) for piece in (hi, mid, lo))

    for h in range(HG_HEADS):
        shift_ref[h, 0, SUBLANES:SUBLANES + chunk, :] = cumsum_rows(gl_ref[h])
        shift_ref[h, 1, SUBLANES:SUBLANES + chunk, :] = k_ref[h]

    def gla_head(h):
        q = q_ref[h]
        k = k_ref[h]
        v = v_ref[h]
        b = shift_ref[h, 0, SUBLANES:SUBLANES + chunk, :]
        qe = q * jnp.exp2(b)
        o = jnp.concatenate([_mm(qe[r], read_hg(i, h)) for i, r in enumerate(seq_rows)], axis=0)

        scores = jnp.zeros((chunk, chunk), F32)
        for dist in range(band):
            if dist == 0:
                p = jnp.sum(q * k, axis=-1, keepdims=True)
            else:
                lo = SUBLANES - dist
                d = b - shift_ref[h, 0, lo:lo + chunk, :]
                if nseq > 1:
                    d = jnp.minimum(d, 0.0)
                p = jnp.sum(q * shift_ref[h, 1, lo:lo + chunk, :] * jnp.exp2(d), axis=-1, keepdims=True)
            scores = jnp.where(class_ref[...] == dist, p, scores)

        for lvl, m in enumerate(levels):
            blk = 2 * m
            bref = jnp.concatenate(
                [jnp.broadcast_to(b[j * blk + m - 1:j * blk + m, :], (blk, HG_KDIM)) for j in range(chunk // blk)],
                axis=0)
            e = jnp.exp2(-jnp.abs(b - bref))
            scores = jnp.where(class_ref[...] == band + lvl, _mm_nt(q * e, k * e), scores)
        o = o + _mm(scores, v)

        for i, r in enumerate(seq_rows):
            b_last = b[r.stop - 1:r.stop, :]
            kd = k[r] * jnp.exp2(b_last - b[r])
            write_hg(i, h, _as_column(jnp.exp2(b_last)) * read_hg(i, h) + _mm_tn(kd, v[r]))

        on = _rms(o, hgw_ref[h]) * go_ref[h]
        ohg_ref[:, h * HG_VDIM:(h + 1) * HG_VDIM] = on.astype(BF16)

    for i, r in enumerate(seq_rows):
        for c in range(CONV_DIM // LANES):
            lanes = slice(c * LANES, (c + 1) * LANES)
            cat_ref[i, c, SUBLANES:SUBLANES + seq_chunk, :] = xbc_ref[r, lanes]
            conv = convb_ref[:, lanes]
            for j in range(CONV_W):
                off = SUBLANES - (CONV_W - 1) + j
                conv = conv + cat_ref[i, c, off:off + seq_chunk, :] * convw_ref[j:j + 1, lanes]
            act_ref[r, lanes] = _silu(conv)
            cat_ref[i, c, 0:SUBLANES, :] = cat_ref[i, c, seq_chunk:seq_chunk + SUBLANES, :]

    lane_t = lax.broadcasted_iota(jnp.int32, (chunk, DT_PAD), 1)
    dt = jnp.where(lane_t < M_HEADS, dt_ref[...], 0.0)
    da = dt * (-LOG2E * jnp.exp(alog_ref[...]))
    cum = cumsum_rows(da)
    key_term = cum.T[0:M_HEADS, :] - jnp.log2(dt.T[0:M_HEADS, :])
    ecum = jnp.exp2(cum)
    lane_x = lax.broadcasted_iota(jnp.int32, (chunk, LANES), 1)
    pairs_per_group = M_GROUP_WIDTH // LANES
    assert HG_HEADS == M_GROUPS * pairs_per_group

    def group_operands(g):
        b_g = act_ref[:, M_DINNER + g * M_DSTATE:M_DINNER + (g + 1) * M_DSTATE]
        c_g = act_ref[:, M_DINNER + (M_GROUPS + g) * M_DSTATE:M_DINNER + (M_GROUPS + g + 1) * M_DSTATE]
        return b_g, c_g, act_ref[:, g * M_GROUP_WIDTH:(g + 1) * M_GROUP_WIDTH]

    cb_groups = [_mm_nt(group_operands(g)[1], group_operands(g)[0]) for g in range(M_GROUPS)]

    def ssd_pair(g, j):
        x_pair = act_ref[:, g * M_GROUP_WIDTH + j * LANES:g * M_GROUP_WIDTH + (j + 1) * LANES]
        causal = class_ref[...] >= 0
        y_pair = None
        for half in range(2):
            hd = g * heads_per_group + 2 * j + half
            diff = (jnp.broadcast_to(cum[:, hd:hd + 1], (chunk, chunk))
                    - jnp.broadcast_to(key_term[hd:hd + 1, :], (chunk, chunk)))
            mh = cb_groups[g] * jnp.exp2(jnp.where(causal, diff, NEG_BIG))
            keep = (lane_x < M_HEADDIM) if half == 0 else (lane_x >= M_HEADDIM)
            part = _mm(mh, jnp.where(keep, x_pair, 0.0))
            y_pair = part if y_pair is None else y_pair + part
        return y_pair

    y_tiles = [[None] * pairs_per_group for _ in range(M_GROUPS)]
    for h in range(HG_HEADS):
        gla_head(h)
        g, j = divmod(h, pairs_per_group)
        y_tiles[g][j] = ssd_pair(g, j)

    y_groups = []
    for g in range(M_GROUPS):
        b_g, c_g, x_g = group_operands(g)
        y_intra = jnp.concatenate(y_tiles[g], axis=1)
        y_inter = jnp.concatenate([_mm_nt(c_g[r], read_ssm(i, g)) for i, r in enumerate(seq_rows)], axis=0)
        y_g = y_intra + y_inter * _expand_heads(ecum, g)
        y_g = y_g + dskip_ref[:, g * M_GROUP_WIDTH:(g + 1) * M_GROUP_WIDTH] * x_g
        y_g = y_g * z_ref[:, g * M_GROUP_WIDTH:(g + 1) * M_GROUP_WIDTH]
        y_groups.append(_rms(y_g, ssmw_ref[:, g * M_GROUP_WIDTH:(g + 1) * M_GROUP_WIDTH]))

        for i, r in enumerate(seq_rows):
            cum_last = cum[r.stop - 1:r.stop, :]
            wgt = dt[r] * jnp.exp2(cum_last - cum[r])
            xw = x_g[r] * _expand_heads(wgt, g)
            decay = jnp.exp2(cum_last)
            decay_rows = jnp.concatenate(
                [jnp.broadcast_to(decay[:, hd:hd + 1], (M_HEADDIM, M_DSTATE))
                 for hd in range(g * heads_per_group, (g + 1) * heads_per_group)], axis=0)
            write_ssm(i, g, decay_rows * read_ssm(i, g) + _mm_tn(xw, b_g[r]))
    yssm_ref[...] = jnp.concatenate(y_groups, axis=1).astype(BF16)

    @when_step(last_step)
    def _store_states():
        for i in range(nseq):
            if carried:
                for h in range(HG_HEADS):
                    hg_out_ref[i, h] = st_hg[i, h]
                for g in range(M_GROUPS):
                    ssm_out_ref[i, group_heads(g)] = st_ssm[i, g].reshape(heads_per_group, M_HEADDIM, M_DSTATE)
            for c in range(CONV_DIM // LANES):
                conv_out_ref[i, :, c * LANES:(c + 1) * LANES] = cat_ref[i, c, 0:SUBLANES, :]


def _mixer(heads5, zs, xbc, dt, hg0, ssm0, conv0, convw, convb, alog, dskip, hgw, ssmw, prev_out, *,
           layer, batch, seq_len):
    if seq_len >= LANES:
        nseq, seq_chunk = 1, LANES
    else:
        nseq, seq_chunk = SAMPLE_SEQS_PER_STEP, seq_len
    assert seq_len % seq_chunk == 0 and batch % nseq == 0 and seq_chunk % max(BAND_LONG, BAND_SHORT) == 0
    chunk = nseq * seq_chunk
    steps = seq_len // seq_chunk
    n = batch * seq_len

    def tok_idx(b, s):
        return b * steps + s

    head_spec = pl.BlockSpec((HG_HEADS, chunk, HG_KDIM), lambda b, s: (0, tok_idx(b, s), 0))

    def tok(width):
        return pl.BlockSpec((chunk, width), lambda b, s: (tok_idx(b, s), 0))

    hg_spec = pl.BlockSpec((None, nseq, HG_HEADS, HG_KDIM, HG_VDIM), lambda b, s: (layer, b, 0, 0, 0))
    ssm_spec = pl.BlockSpec((None, nseq, M_HEADS, M_HEADDIM, M_DSTATE), lambda b, s: (layer, b, 0, 0, 0))
    conv_spec = pl.BlockSpec((None, nseq, SUBLANES, CONV_DIM), lambda b, s: (layer, b, 0, 0))
    prev_out = () if prev_out is None else tuple(prev_out)
    first_state_out = 2
    assert len(prev_out) in (0, 3)
    aliases = {_MIXER_INPUTS + j: first_state_out + j for j in range(len(prev_out))}

    return pl.pallas_call(
        functools.partial(_mixer_body, nseq, seq_chunk, steps),
        out_shape=(
            jax.ShapeDtypeStruct((n, HG_WIDTH), BF16),
            jax.ShapeDtypeStruct((n, M_DINNER), BF16),
            jax.ShapeDtypeStruct(hg0.shape, F32),
            jax.ShapeDtypeStruct(ssm0.shape, F32),
            jax.ShapeDtypeStruct(conv0.shape, F32),
        ),
        grid=(batch // nseq, steps),
        in_specs=[head_spec] * 5 + [tok(M_DINNER), tok(CONV_DIM), tok(DT_PAD), hg_spec, ssm_spec, conv_spec,
                                    _resident((CONV_W, CONV_DIM)), _resident((1, CONV_DIM)), _resident((1, DT_PAD)),
                                    _resident((1, M_DINNER)), _resident((HG_HEADS, 1, HG_VDIM)), _resident((1, M_DINNER))]
        + [pl.BlockSpec(memory_space=pl.ANY)] * len(prev_out),
        out_specs=(tok(HG_WIDTH), tok(M_DINNER), hg_spec, ssm_spec, conv_spec),
        input_output_aliases=aliases,
        scratch_shapes=[
            pltpu.VMEM((nseq if steps > 1 else 0, HG_HEADS, HG_KDIM, HG_VDIM), F32),
            pltpu.VMEM((nseq if steps > 1 else 0, M_GROUPS, M_GROUP_WIDTH, M_DSTATE), F32),
            pltpu.VMEM((nseq, CONV_DIM // LANES, SUBLANES + seq_chunk, LANES), F32),
            pltpu.VMEM((HG_HEADS, 2, SUBLANES + chunk, HG_KDIM), F32),
            pltpu.VMEM((chunk, chunk), BF16),
            pltpu.VMEM((chunk, chunk), jnp.int32),
            pltpu.VMEM((chunk, CONV_DIM), F32),
        ],
        compiler_params=pltpu.CompilerParams(dimension_semantics=("arbitrary", "arbitrary"),
                                             vmem_limit_bytes=VMEM_LIMIT),
        name="mixer",
    )(*heads5, zs, xbc, dt, hg0, ssm0, conv0, convw, convb, alog, dskip, hgw, ssmw, *prev_out)


def _outproj_body(h_ref, ohg_ref, yssm_ref, g_ref, w_ref, o_ref):
    m = jnp.dot(ohg_ref[...], w_ref[0:HG_WIDTH, :], preferred_element_type=F32)
    m = m + jnp.dot(yssm_ref[...], w_ref[HG_WIDTH:HG_WIDTH + M_DINNER, :], preferred_element_type=F32)
    o_ref[...] = h_ref[...] + _rms(m, g_ref[...])


def _outproj(h, ohg, yssm, gain, w, layer):
    n = h.shape[0]
    tm = _token_tile(n, largest=OUTPROJ_TILE)

    def tok(width):
        return pl.BlockSpec((tm, width), lambda i: (i, 0))

    return pl.pallas_call(
        _outproj_body,
        out_shape=jax.ShapeDtypeStruct((n, D_MODEL), F32),
        grid=(n // tm,),
        in_specs=[tok(D_MODEL), tok(HG_WIDTH), tok(M_DINNER), _resident((1, D_MODEL)),
                  _resident_layer(w, layer)],
        out_specs=tok(D_MODEL),
        compiler_params=pltpu.CompilerParams(dimension_semantics=("arbitrary",), vmem_limit_bytes=VMEM_LIMIT),
        name="outproj",
    )(h, ohg, yssm, gain, w)


def _prep_layers(w_in, *rest):
    w_main, w_dt = _prep_w_in(w_in)
    return [_prep_layer(l, w_main, w_dt, *rest) for l in range(DEPTH)]


def _prep_layer(l, w_main, w_dt, hg_lb_logits, conv_w, conv_b, dt_bias, a_log, d_skip, hg_norm_w, ssm_norm_w, w_out,
                f1g, f1u, f1d, f2g, f2u, f2d, norm_gain):
    pad = DT_PAD - M_HEADS
    return dict(
        layer=l,
        w_in=w_main,
        w_dt=w_dt,
        lb_logits=hg_lb_logits,
        conv_w=conv_w[l],
        conv_b=conv_b[l][None, :],
        dt_bias=jnp.pad(dt_bias[l], (0, pad))[None, :],
        a_log=jnp.pad(a_log[l], (0, pad))[None, :],
        d_skip=jnp.repeat(d_skip[l], M_HEADDIM)[None, :],
        hg_norm_w=hg_norm_w[l].reshape(HG_HEADS, 1, HG_VDIM),
        ssm_norm_w=ssm_norm_w[l][None, :],
        w_out=w_out.astype(BF16),
        ffn1=(f1g.astype(BF16), f1u.astype(BF16), f1d.astype(BF16)),
        ffn2=(f2g.astype(BF16), f2u.astype(BF16), f2d.astype(BF16)),
        gains=norm_gain[l],
    )


def _layer(x, hg0, ssm0, conv0, prev_out, p, *, batch, seq_len):
    gains = p["gains"]
    h = _ffn(x, gains[0:2], *p["ffn1"], p["layer"])
    *heads5, zs, xbc, dt = _inproj(h, gains[2:3], p["lb_logits"], p["dt_bias"], p["w_in"], p["w_dt"], p["layer"])
    ohg, yssm, *states = _mixer(
        heads5, zs, xbc, dt, hg0, ssm0, conv0, p["conv_w"], p["conv_b"], p["a_log"], p["d_skip"],
        p["hg_norm_w"], p["ssm_norm_w"], prev_out, layer=p["layer"], batch=batch, seq_len=seq_len)
    h = _outproj(h, ohg, yssm, gains[3:4], p["w_out"], p["layer"])
    h = _ffn(h, gains[4:6], *p["ffn2"], p["layer"])
    return h, states


def _trunk(x, s_hg, s_ssm, s_conv, layers):
    batch, seq_len, _ = x.shape
    h = x.reshape(batch * seq_len, D_MODEL)
    conv_pad = jnp.pad(s_conv, ((0, 0), (0, 0), (SUBLANES - (CONV_W - 1), 0), (0, 0)))
    states = None
    for p in layers:
        h, states = _layer(h, s_hg, s_ssm, conv_pad, states, p, batch=batch, seq_len=seq_len)
    hg_out, ssm_out, conv_out = states
    return h.reshape(batch, seq_len, D_MODEL), hg_out, ssm_out, conv_out[:, :, SUBLANES - (CONV_W - 1):, :]


def kernel(x_prompt, x_sample, state_hgrn, state_ssm, state_conv, w_in, hg_lb_logits, conv_w, conv_b, dt_bias, a_log, d_skip, hg_norm_w, ssm_norm_w, w_out, ffn1_w_gate, ffn1_w_up, ffn1_w_down, ffn2_w_gate, ffn2_w_up, ffn2_w_down, norm_gain):
    layers = _prep_layers(w_in, hg_lb_logits, conv_w, conv_b, dt_bias, a_log, d_skip, hg_norm_w, ssm_norm_w, w_out,
                          ffn1_w_gate, ffn1_w_up, ffn1_w_down, ffn2_w_gate, ffn2_w_up, ffn2_w_down, norm_gain)
    bp = x_prompt.shape[0]
    y_prompt, hg_p, ssm_p, conv_p = _trunk(
        x_prompt,
        jnp.zeros((DEPTH, bp, HG_HEADS, HG_KDIM, HG_VDIM), F32),
        jnp.zeros((DEPTH, bp, M_HEADS, M_HEADDIM, M_DSTATE), F32),
        jnp.zeros((DEPTH, bp, CONV_W - 1, CONV_DIM), F32),
        layers)
    y_sample, hg_s, ssm_s, conv_s = _trunk(x_sample, state_hgrn, state_ssm, state_conv, layers)
    return (y_prompt, y_sample, hg_p, ssm_p, conv_p, hg_s, ssm_s, conv_s)
```

```python
import functools

import jax
import jax.numpy as jnp
from jax import lax
from jax.experimental import pallas as pl
from jax.experimental.pallas import tpu as pltpu

F32 = jnp.float32
BF16 = jnp.bfloat16

D_MODEL = 1024
DEPTH = 2
HG_HEADS = 8
HG_KDIM = 128
HG_VDIM = 128
HG_WIDTH = HG_HEADS * HG_KDIM
LB_FLOOR = 1e-30
M_DINNER = 1024
M_HEADDIM = 64
M_HEADS = 16
M_DSTATE = 128
M_GROUPS = 2
M_GROUP_WIDTH = M_DINNER // M_GROUPS
CONV_W = 4
CONV_DIM = M_DINNER + 2 * M_GROUPS * M_DSTATE
D_FF = 2816
EPS = 1e-6

LANES = 128
SUBLANES = 8
DT_PAD = LANES
D_PROJ_MAIN = 4 * HG_WIDTH + M_DINNER + CONV_DIM
FFN_TILE = 512
OUTPROJ_TILE = 512
INPROJ_TILE = 512
INPROJ_COLS = 256
SAMPLE_SEQS_PER_STEP = 8
LOG2E = 1.4426950408889634
NEG_BIG = -1e30
VMEM_LIMIT = 56 * 1024 * 1024

_NT = (((1,), (1,)), ((), ()))
_TN = (((0,), (0,)), ((), ()))


def _rms(x, w):
    return x * lax.rsqrt(jnp.mean(x * x, axis=-1, keepdims=True) + EPS) * w


def _silu(x):
    return x * (1.0 / (1.0 + jnp.exp2(x * (-LOG2E))))


def _mm(a, b):
    return jnp.dot(a.astype(BF16), b.astype(BF16), preferred_element_type=F32)


def _mm_nt(a, b):
    return lax.dot_general(a.astype(BF16), b.astype(BF16), _NT, preferred_element_type=F32)


def _mm_tn(a, b):
    return lax.dot_general(a.astype(BF16), b.astype(BF16), _TN, preferred_element_type=F32)


def _resident(shape):
    nd = len(shape)
    return pl.BlockSpec(shape, lambda *_: (0,) * nd, pipeline_mode=pl.Buffered(1))


def _resident_layer(stacked, layer):
    return pl.BlockSpec((None,) + stacked.shape[1:], lambda *_: (layer, 0, 0), pipeline_mode=pl.Buffered(1))


def _token_tile(n_tokens, largest):
    for tm in (t for t in (1024, 512, 256, 128, 64, 32, 16, 8) if t <= largest):
        if n_tokens % tm == 0:
            return tm
    raise ValueError(f"token count {n_tokens} is not a multiple of 8")


FF_CHUNK = 256


def _ffn_body(x_ref, g_ref, wg_ref, wu_ref, wd_ref, o_ref, act_ref):
    x = x_ref[...]
    xn = _rms(x, g_ref[0:1, :]).astype(BF16)
    for j in range(D_FF // FF_CHUNK):
        sl = slice(j * FF_CHUNK, (j + 1) * FF_CHUNK)
        gate = jnp.dot(xn, wg_ref[:, sl], preferred_element_type=F32)
        up = jnp.dot(xn, wu_ref[:, sl], preferred_element_type=F32)
        act_ref[:, sl] = (_silu(gate) * up).astype(BF16)
    down = jnp.dot(act_ref[...], wd_ref[...], preferred_element_type=F32)
    o_ref[...] = x + 0.5 * _rms(down, g_ref[1:2, :])


def _ffn(x, gains2, wg, wu, wd, layer):
    n = x.shape[0]
    tm = _token_tile(n, largest=FFN_TILE)
    return pl.pallas_call(
        _ffn_body,
        out_shape=jax.ShapeDtypeStruct((n, D_MODEL), F32),
        grid=(n // tm,),
        in_specs=[
            pl.BlockSpec((tm, D_MODEL), lambda i: (i, 0)),
            _resident((2, D_MODEL)),
            _resident_layer(wg, layer),
            _resident_layer(wu, layer),
            _resident_layer(wd, layer),
        ],
        out_specs=pl.BlockSpec((tm, D_MODEL), lambda i: (i, 0)),
        scratch_shapes=[pltpu.VMEM((tm, D_FF), BF16)],
        compiler_params=pltpu.CompilerParams(dimension_semantics=("arbitrary",), vmem_limit_bytes=VMEM_LIMIT),
        name="ffn",
    )(x, gains2, wg, wu, wd)


W_PREP_ROWS = 512


def _transpose_cast_body(wt_ref, o_ref):
    o_ref[...] = wt_ref[...].T.astype(BF16)


def _transpose_pad_body(wt_ref, o_ref):
    o_ref[...] = jnp.zeros(o_ref.shape, F32)
    o_ref[:, 0:M_HEADS] = wt_ref[...].T


def _prep_w_in(w_in):
    wt = jnp.swapaxes(w_in, 1, 2)
    w_dt = pl.pallas_call(
        _transpose_pad_body,
        out_shape=jax.ShapeDtypeStruct((DEPTH, D_MODEL, DT_PAD), F32),
        grid=(DEPTH,),
        in_specs=[pl.BlockSpec((None, M_HEADS, D_MODEL), lambda l: (l, D_PROJ_MAIN // M_HEADS, 0))],
        out_specs=pl.BlockSpec((None, D_MODEL, DT_PAD), lambda l: (l, 0, 0)),
        compiler_params=pltpu.CompilerParams(dimension_semantics=("arbitrary",)),
        name="w_dt_relayout",
    )(wt)
    w_main = pl.pallas_call(
        _transpose_cast_body,
        out_shape=jax.ShapeDtypeStruct((DEPTH, D_MODEL, D_PROJ_MAIN), BF16),
        grid=(DEPTH, D_PROJ_MAIN // W_PREP_ROWS),
        in_specs=[pl.BlockSpec((None, W_PREP_ROWS, D_MODEL), lambda l, j: (l, j, 0))],
        out_specs=pl.BlockSpec((None, D_MODEL, W_PREP_ROWS), lambda l, j: (l, 0, j)),
        compiler_params=pltpu.CompilerParams(dimension_semantics=("arbitrary", "arbitrary")),
        name="w_in_relayout",
    )(wt)
    return w_main, w_dt


_C_Q = 0
_C_F = HG_WIDTH
_C_I = 2 * HG_WIDTH
_C_G = 3 * HG_WIDTH
_C_Z = 4 * HG_WIDTH
_C_X = _C_Z + M_DINNER


def _inproj_body(layer, x_ref, g_ref, lbl_ref, dtb_ref, w_ref, wdt_ref,
                 q_ref, k_ref, gl_ref, v_ref, go_ref, z_ref, xbc_ref, dt_ref, xn_ref):
    xn_ref[...] = _rms(x_ref[...], g_ref[...]).astype(BF16)

    def proj(c0, width):
        return jnp.dot(xn_ref[...], w_ref[:, c0:c0 + width], preferred_element_type=F32)

    lg = lbl_ref[...]
    e = jnp.exp(lg - jnp.max(lg, axis=0, keepdims=True))
    p = e / jnp.sum(e, axis=0, keepdims=True)
    lb = jnp.sum(p[0:layer + 1, :], axis=0, keepdims=True) - p[0:1, :]
    lb_floor = jnp.maximum(lb, LB_FLOOR)

    def head_store(ref, c, val):
        for j in range(INPROJ_COLS // HG_KDIM):
            ref[c // HG_KDIM + j] = val[:, j * HG_KDIM:(j + 1) * HG_KDIM].astype(ref.dtype)

    def gate_block(c):
        cols = slice(c, c + INPROJ_COLS)
        fr = proj(_C_F + c, INPROJ_COLS)
        t = jnp.exp(-jnp.abs(fr))
        big = 1.0 / (1.0 + t)
        small = t * big
        positive = fr >= 0.0
        kk = (1.0 - lb[:, cols]) * jnp.where(positive, small, big)
        f_direct = lb_floor[:, cols] + (1.0 - lb[:, cols]) * jnp.where(positive, big, small)
        glog = jnp.where(positive, jnp.log1p((lb_floor[:, cols] - lb[:, cols]) - kk), jnp.log(f_direct))
        head_store(k_ref, c, kk)
        head_store(gl_ref, c, glog * LOG2E)

    def xbc_block(c):
        xbc_ref[:, c:c + INPROJ_COLS] = proj(_C_X + c, INPROJ_COLS)

    xbc_blocks = list(range(0, CONV_DIM, INPROJ_COLS))
    for c in range(0, HG_WIDTH, INPROJ_COLS):
        head_store(q_ref, c, proj(_C_Q + c, INPROJ_COLS) * (HG_KDIM ** -0.5))
        gate_block(c)
        xbc_block(xbc_blocks.pop())
        head_store(v_ref, c, proj(_C_I + c, INPROJ_COLS))
        head_store(go_ref, c, _silu(proj(_C_G + c, INPROJ_COLS)))
        z_ref[:, c:c + INPROJ_COLS] = _silu(proj(_C_Z + c, INPROJ_COLS)).astype(BF16)
    for c in xbc_blocks:
        xbc_block(c)
    dt_ref[...] = jax.nn.softplus(
        jnp.dot(xn_ref[...], wdt_ref[...].astype(BF16), preferred_element_type=F32) + dtb_ref[...])


def _inproj(h, gain, lb_logits, dt_bias_pad, w, w_dt, layer):
    n = h.shape[0]
    tm = _token_tile(n, largest=INPROJ_TILE)

    def heads(dtype):
        return jax.ShapeDtypeStruct((HG_HEADS, n, HG_KDIM), dtype)

    head_spec = pl.BlockSpec((HG_HEADS, tm, HG_KDIM), lambda i: (0, i, 0))

    def tok(width):
        return pl.BlockSpec((tm, width), lambda i: (i, 0))

    return pl.pallas_call(
        functools.partial(_inproj_body, layer),
        out_shape=(heads(F32), heads(F32), heads(F32), heads(BF16), heads(BF16),
                   jax.ShapeDtypeStruct((n, M_DINNER), BF16),
                   jax.ShapeDtypeStruct((n, CONV_DIM), F32),
                   jax.ShapeDtypeStruct((n, DT_PAD), F32)),
        grid=(n // tm,),
        in_specs=[
            tok(D_MODEL),
            _resident((1, D_MODEL)),
            _resident((DEPTH, HG_WIDTH)),
            _resident((1, DT_PAD)),
            _resident_layer(w, layer),
            _resident_layer(w_dt, layer),
        ],
        out_specs=(head_spec,) * 5 + (tok(M_DINNER), tok(CONV_DIM), tok(DT_PAD)),
        scratch_shapes=[pltpu.VMEM((tm, D_MODEL), BF16)],
        compiler_params=pltpu.CompilerParams(dimension_semantics=("arbitrary",), vmem_limit_bytes=VMEM_LIMIT),
        name="inproj",
    )(h, gain, lb_logits, dt_bias_pad, w, w_dt)


BAND_LONG = 4
BAND_SHORT = 2


def _expand_heads(vals, group):
    rows = vals.shape[0]
    lane = lax.broadcasted_iota(jnp.int32, (rows, LANES), 1)
    tiles = []
    for j in range(M_GROUP_WIDTH // LANES):
        h0 = group * (M_HEADS // M_GROUPS) + 2 * j
        lo = jnp.broadcast_to(vals[:, h0:h0 + 1], (rows, LANES))
        hi = jnp.broadcast_to(vals[:, h0 + 1:h0 + 2], (rows, LANES))
        tiles.append(jnp.where(lane < M_HEADDIM, lo, hi))
    return jnp.concatenate(tiles, axis=1)


def _as_column(row_vec):
    n = row_vec.shape[1]
    hi = row_vec.astype(BF16).astype(F32)
    mid = (row_vec - hi).astype(BF16).astype(F32)
    lo = row_vec - hi - mid
    sub = lax.broadcasted_iota(jnp.int32, (SUBLANES, n), 0)
    pieces = jnp.where(sub == 0, hi, jnp.where(sub == 1, mid, jnp.where(sub == 2, lo, 0.0)))
    return _mm_tn(pieces, jnp.ones((SUBLANES, n), F32))


_MIXER_INPUTS = 17


def _mixer_body(nseq, seq_chunk, steps, *refs):
    band = BAND_LONG if nseq == 1 else BAND_SHORT
    (q_ref, k_ref, gl_ref, v_ref, go_ref, z_ref, xbc_ref, dt_ref, hg0_ref, ssm0_ref, conv0_ref,
     convw_ref, convb_ref, alog_ref, dskip_ref, hgw_ref, ssmw_ref) = refs[:_MIXER_INPUTS]
    (ohg_ref, yssm_ref, hg_out_ref, ssm_out_ref, conv_out_ref,
     st_hg, st_ssm, cat_ref, shift_ref, mask_ref, class_ref, act_ref) = refs[-12:]
    chunk = nseq * seq_chunk
    step = pl.program_id(1)
    last_step = steps - 1
    heads_per_group = M_HEADS // M_GROUPS
    carried = steps > 1

    def when_step(which):
        if carried:
            return pl.when(step == which)
        return lambda block: block()

    def group_heads(g):
        return slice(g * heads_per_group, (g + 1) * heads_per_group)

    def read_hg(i, h):
        return st_hg[i, h] if carried else hg0_ref[i, h]

    def write_hg(i, h, val):
        if carried:
            st_hg[i, h] = val
        else:
            hg_out_ref[i, h] = val

    def read_ssm(i, g):
        return st_ssm[i, g] if carried else ssm0_ref[i, group_heads(g)].reshape(M_GROUP_WIDTH, M_DSTATE)

    def write_ssm(i, g, val):
        if carried:
            st_ssm[i, g] = val
        else:
            ssm_out_ref[i, group_heads(g)] = val.reshape(heads_per_group, M_HEADDIM, M_DSTATE)

    @when_step(0)
    def _load_states():
        for i in range(nseq):
            if carried:
                for h in range(HG_HEADS):
                    st_hg[i, h] = hg0_ref[i, h]
                for g in range(M_GROUPS):
                    st_ssm[i, g] = ssm0_ref[i, group_heads(g)].reshape(M_GROUP_WIDTH, M_DSTATE)
            for c in range(CONV_DIM // LANES):
                cat_ref[i, c, 0:SUBLANES, :] = conv0_ref[i, :, c * LANES:(c + 1) * LANES]
        shift_ref[:, :, 0:SUBLANES, :] = jnp.zeros((HG_HEADS, 2, SUBLANES, HG_KDIM), F32)

    levels = []
    m = band
    while 2 * m <= seq_chunk:
        levels.append(m)
        m *= 2

    @when_step(0)
    def _build_masks():
        row = lax.broadcasted_iota(jnp.int32, (chunk, chunk), 0)
        col = lax.broadcasted_iota(jnp.int32, (chunk, chunk), 1)
        causal = (col <= row) & ((row // seq_chunk) == (col // seq_chunk))
        mask_ref[...] = causal.astype(F32).astype(BF16)
        pair_class = jnp.full((chunk, chunk), -1, jnp.int32)
        for lvl in reversed(range(len(levels))):
            blk_log2 = jnp.int32((2 * levels[lvl]).bit_length() - 1)
            same_blk = lax.shift_right_logical(row, blk_log2) == lax.shift_right_logical(col, blk_log2)
            pair_class = jnp.where(same_blk, band + lvl, pair_class)
        pair_class = jnp.where(row - col < band, row - col, pair_class)
        class_ref[...] = jnp.where(causal, pair_class, -1)

    seq_rows = [slice(i * seq_chunk, (i + 1) * seq_chunk) for i in range(nseq)]

    def cumsum_rows(x):
        hi = x.astype(BF16)
        rest = x - hi.astype(F32)
        mid = rest.astype(BF16)
        lo = (rest - mid.astype(F32)).astype(BF16)
        return sum(jnp.dot(mask_ref[...], piece, preferred_element_type=F32) for piece in (hi, mid, lo))

    for h in range(HG_HEADS):
        shift_ref[h, 0, SUBLANES:SUBLANES + chunk, :] = cumsum_rows(gl_ref[h])
        shift_ref[h, 1, SUBLANES:SUBLANES + chunk, :] = k_ref[h]

    def gla_head(h):
        q = q_ref[h]
        k = k_ref[h]
        v = v_ref[h]
        b = shift_ref[h, 0, SUBLANES:SUBLANES + chunk, :]
        qe = q * jnp.exp2(b)
        o = jnp.concatenate([_mm(qe[r], read_hg(i, h)) for i, r in enumerate(seq_rows)], axis=0)

        scores = jnp.zeros((chunk, chunk), F32)
        for dist in range(band):
            if dist == 0:
                p = jnp.sum(q * k, axis=-1, keepdims=True)
            else:
                lo = SUBLANES - dist
                d = b - shift_ref[h, 0, lo:lo + chunk, :]
                if nseq > 1:
                    d = jnp.minimum(d, 0.0)
                p = jnp.sum(q * shift_ref[h, 1, lo:lo + chunk, :] * jnp.exp2(d), axis=-1, keepdims=True)
            scores = jnp.where(class_ref[...] == dist, p, scores)

        for lvl, m in enumerate(levels):
            blk = 2 * m
            bref = jnp.concatenate(
                [jnp.broadcast_to(b[j * blk + m - 1:j * blk + m, :], (blk, HG_KDIM)) for j in range(chunk // blk)],
                axis=0)
            e = jnp.exp2(-jnp.abs(b - bref))
            scores = jnp.where(class_ref[...] == band + lvl, _mm_nt(q * e, k * e), scores)
        o = o + _mm(scores, v)

        for i, r in enumerate(seq_rows):
            b_last = b[r.stop - 1:r.stop, :]
            kd = k[r] * jnp.exp2(b_last - b[r])
            write_hg(i, h, _as_column(jnp.exp2(b_last)) * read_hg(i, h) + _mm_tn(kd, v[r]))

        on = _rms(o, hgw_ref[h]) * go_ref[h]
        ohg_ref[:, h * HG_VDIM:(h + 1) * HG_VDIM] = on.astype(BF16)

    for i, r in enumerate(seq_rows):
        for c in range(CONV_DIM // LANES):
            lanes = slice(c * LANES, (c + 1) * LANES)
            cat_ref[i, c, SUBLANES:SUBLANES + seq_chunk, :] = xbc_ref[r, lanes]
            conv = convb_ref[:, lanes]
            for j in range(CONV_W):
                off = SUBLANES - (CONV_W - 1) + j
                conv = conv + cat_ref[i, c, off:off + seq_chunk, :] * convw_ref[j:j + 1, lanes]
            act_ref[r, lanes] = _silu(conv)
            cat_ref[i, c, 0:SUBLANES, :] = cat_ref[i, c, seq_chunk:seq_chunk + SUBLANES, :]

    lane_t = lax.broadcasted_iota(jnp.int32, (chunk, DT_PAD), 1)
    dt = jnp.where(lane_t < M_HEADS, dt_ref[...], 0.0)
    da = dt * (-LOG2E * jnp.exp(alog_ref[...]))
    cum = cumsum_rows(da)
    key_term = cum.T[0:M_HEADS, :] - jnp.log2(dt.T[0:M_HEADS, :])
    ecum = jnp.exp2(cum)
    lane_x = lax.broadcasted_iota(jnp.int32, (chunk, LANES), 1)
    pairs_per_group = M_GROUP_WIDTH // LANES
    assert HG_HEADS == M_GROUPS * pairs_per_group

    def group_operands(g):
        b_g = act_ref[:, M_DINNER + g * M_DSTATE:M_DINNER + (g + 1) * M_DSTATE]
        c_g = act_ref[:, M_DINNER + (M_GROUPS + g) * M_DSTATE:M_DINNER + (M_GROUPS + g + 1) * M_DSTATE]
        return b_g, c_g, act_ref[:, g * M_GROUP_WIDTH:(g + 1) * M_GROUP_WIDTH]

    cb_groups = [_mm_nt(group_operands(g)[1], group_operands(g)[0]) for g in range(M_GROUPS)]

    def ssd_pair(g, j):
        x_pair = act_ref[:, g * M_GROUP_WIDTH + j * LANES:g * M_GROUP_WIDTH + (j + 1) * LANES]
        causal = class_ref[...] >= 0
        y_pair = None
        for half in range(2):
            hd = g * heads_per_group + 2 * j + half
            diff = (jnp.broadcast_to(cum[:, hd:hd + 1], (chunk, chunk))
                    - jnp.broadcast_to(key_term[hd:hd + 1, :], (chunk, chunk)))
            mh = cb_groups[g] * jnp.exp2(jnp.where(causal, diff, NEG_BIG))
            keep = (lane_x < M_HEADDIM) if half == 0 else (lane_x >= M_HEADDIM)
            part = _mm(mh, jnp.where(keep, x_pair, 0.0))
            y_pair = part if y_pair is None else y_pair + part
        return y_pair

    y_tiles = [[None] * pairs_per_group for _ in range(M_GROUPS)]
    for h in range(HG_HEADS):
        gla_head(h)
        g, j = divmod(h, pairs_per_group)
        y_tiles[g][j] = ssd_pair(g, j)

    y_groups = []
    for g in range(M_GROUPS):
        b_g, c_g, x_g = group_operands(g)
        y_intra = jnp.concatenate(y_tiles[g], axis=1)
        y_inter = jnp.concatenate([_mm_nt(c_g[r], read_ssm(i, g)) for i, r in enumerate(seq_rows)], axis=0)
        y_g = y_intra + y_inter * _expand_heads(ecum, g)
        y_g = y_g + dskip_ref[:, g * M_GROUP_WIDTH:(g + 1) * M_GROUP_WIDTH] * x_g
        y_g = y_g * z_ref[:, g * M_GROUP_WIDTH:(g + 1) * M_GROUP_WIDTH]
        y_groups.append(_rms(y_g, ssmw_ref[:, g * M_GROUP_WIDTH:(g + 1) * M_GROUP_WIDTH]))

        for i, r in enumerate(seq_rows):
            cum_last = cum[r.stop - 1:r.stop, :]
            wgt = dt[r] * jnp.exp2(cum_last - cum[r])
            xw = x_g[r] * _expand_heads(wgt, g)
            decay = jnp.exp2(cum_last)
            decay_rows = jnp.concatenate(
                [jnp.broadcast_to(decay[:, hd:hd + 1], (M_HEADDIM, M_DSTATE))
                 for hd in range(g * heads_per_group, (g + 1) * heads_per_group)], axis=0)
            write_ssm(i, g, decay_rows * read_ssm(i, g) + _mm_tn(xw, b_g[r]))
    yssm_ref[...] = jnp.concatenate(y_groups, axis=1).astype(BF16)

    @when_step(last_step)
    def _store_states():
        for i in range(nseq):
            if carried:
                for h in range(HG_HEADS):
                    hg_out_ref[i, h] = st_hg[i, h]
                for g in range(M_GROUPS):
                    ssm_out_ref[i, group_heads(g)] = st_ssm[i, g].reshape(heads_per_group, M_HEADDIM, M_DSTATE)
            for c in range(CONV_DIM // LANES):
                conv_out_ref[i, :, c * LANES:(c + 1) * LANES] = cat_ref[i, c, 0:SUBLANES, :]


def _mixer(heads5, zs, xbc, dt, hg0, ssm0, conv0, convw, convb, alog, dskip, hgw, ssmw, prev_out, *,
           layer, batch, seq_len):
    if seq_len >= LANES:
        nseq, seq_chunk = 1, LANES
    else:
        nseq, seq_chunk = SAMPLE_SEQS_PER_STEP, seq_len
    assert seq_len % seq_chunk == 0 and batch % nseq == 0 and seq_chunk % max(BAND_LONG, BAND_SHORT) == 0
    chunk = nseq * seq_chunk
    steps = seq_len // seq_chunk
    n = batch * seq_len

    def tok_idx(b, s):
        return b * steps + s

    head_spec = pl.BlockSpec((HG_HEADS, chunk, HG_KDIM), lambda b, s: (0, tok_idx(b, s), 0))

    def tok(width):
        return pl.BlockSpec((chunk, width), lambda b, s: (tok_idx(b, s), 0))

    hg_spec = pl.BlockSpec((None, nseq, HG_HEADS, HG_KDIM, HG_VDIM), lambda b, s: (layer, b, 0, 0, 0))
    ssm_spec = pl.BlockSpec((None, nseq, M_HEADS, M_HEADDIM, M_DSTATE), lambda b, s: (layer, b, 0, 0, 0))
    conv_spec = pl.BlockSpec((None, nseq, SUBLANES, CONV_DIM), lambda b, s: (layer, b, 0, 0))
    prev_out = () if prev_out is None else tuple(prev_out)
    first_state_out = 2
    assert len(prev_out) in (0, 3)
    aliases = {_MIXER_INPUTS + j: first_state_out + j for j in range(len(prev_out))}

    return pl.pallas_call(
        functools.partial(_mixer_body, nseq, seq_chunk, steps),
        out_shape=(
            jax.ShapeDtypeStruct((n, HG_WIDTH), BF16),
            jax.ShapeDtypeStruct((n, M_DINNER), BF16),
            jax.ShapeDtypeStruct(hg0.shape, F32),
            jax.ShapeDtypeStruct(ssm0.shape, F32),
            jax.ShapeDtypeStruct(conv0.shape, F32),
        ),
        grid=(batch // nseq, steps),
        in_specs=[head_spec] * 5 + [tok(M_DINNER), tok(CONV_DIM), tok(DT_PAD), hg_spec, ssm_spec, conv_spec,
                                    _resident((CONV_W, CONV_DIM)), _resident((1, CONV_DIM)), _resident((1, DT_PAD)),
                                    _resident((1, M_DINNER)), _resident((HG_HEADS, 1, HG_VDIM)), _resident((1, M_DINNER))]
        + [pl.BlockSpec(memory_space=pl.ANY)] * len(prev_out),
        out_specs=(tok(HG_WIDTH), tok(M_DINNER), hg_spec, ssm_spec, conv_spec),
        input_output_aliases=aliases,
        scratch_shapes=[
            pltpu.VMEM((nseq if steps > 1 else 0, HG_HEADS, HG_KDIM, HG_VDIM), F32),
            pltpu.VMEM((nseq if steps > 1 else 0, M_GROUPS, M_GROUP_WIDTH, M_DSTATE), F32),
            pltpu.VMEM((nseq, CONV_DIM // LANES, SUBLANES + seq_chunk, LANES), F32),
            pltpu.VMEM((HG_HEADS, 2, SUBLANES + chunk, HG_KDIM), F32),
            pltpu.VMEM((chunk, chunk), BF16),
            pltpu.VMEM((chunk, chunk), jnp.int32),
            pltpu.VMEM((chunk, CONV_DIM), F32),
        ],
        compiler_params=pltpu.CompilerParams(dimension_semantics=("arbitrary", "arbitrary"),
                                             vmem_limit_bytes=VMEM_LIMIT),
        name="mixer",
    )(*heads5, zs, xbc, dt, hg0, ssm0, conv0, convw, convb, alog, dskip, hgw, ssmw, *prev_out)


def _outproj_body(h_ref, ohg_ref, yssm_ref, g_ref, w_ref, o_ref):
    m = jnp.dot(ohg_ref[...], w_ref[0:HG_WIDTH, :], preferred_element_type=F32)
    m = m + jnp.dot(yssm_ref[...], w_ref[HG_WIDTH:HG_WIDTH + M_DINNER, :], preferred_element_type=F32)
    o_ref[...] = h_ref[...] + _rms(m, g_ref[...])


def _outproj(h, ohg, yssm, gain, w, layer):
    n = h.shape[0]
    tm = _token_tile(n, largest=OUTPROJ_TILE)

    def tok(width):
        return pl.BlockSpec((tm, width), lambda i: (i, 0))

    return pl.pallas_call(
        _outproj_body,
        out_shape=jax.ShapeDtypeStruct((n, D_MODEL), F32),
        grid=(n // tm,),
        in_specs=[tok(D_MODEL), tok(HG_WIDTH), tok(M_DINNER), _resident((1, D_MODEL)),
                  _resident_layer(w, layer)],
        out_specs=tok(D_MODEL),
        compiler_params=pltpu.CompilerParams(dimension_semantics=("arbitrary",), vmem_limit_bytes=VMEM_LIMIT),
        name="outproj",
    )(h, ohg, yssm, gain, w)


def _prep_layers(w_in, *rest):
    w_main, w_dt = _prep_w_in(w_in)
    return [_prep_layer(l, w_main, w_dt, *rest) for l in range(DEPTH)]


def _prep_layer(l, w_main, w_dt, hg_lb_logits, conv_w, conv_b, dt_bias, a_log, d_skip, hg_norm_w, ssm_norm_w, w_out,
                f1g, f1u, f1d, f2g, f2u, f2d, norm_gain):
    pad = DT_PAD - M_HEADS
    return dict(
        layer=l,
        w_in=w_main,
        w_dt=w_dt,
        lb_logits=hg_lb_logits,
        conv_w=conv_w[l],
        conv_b=conv_b[l][None, :],
        dt_bias=jnp.pad(dt_bias[l], (0, pad))[None, :],
        a_log=jnp.pad(a_log[l], (0, pad))[None, :],
        d_skip=jnp.repeat(d_skip[l], M_HEADDIM)[None, :],
        hg_norm_w=hg_norm_w[l].reshape(HG_HEADS, 1, HG_VDIM),
        ssm_norm_w=ssm_norm_w[l][None, :],
        w_out=w_out.astype(BF16),
        ffn1=(f1g.astype(BF16), f1u.astype(BF16), f1d.astype(BF16)),
        ffn2=(f2g.astype(BF16), f2u.astype(BF16), f2d.astype(BF16)),
        gains=norm_gain[l],
    )


def _layer(x, hg0, ssm0, conv0, prev_out, p, *, batch, seq_len):
    gains = p["gains"]
    h = _ffn(x, gains[0:2], *p["ffn1"], p["layer"])
    *heads5, zs, xbc, dt = _inproj(h, gains[2:3], p["lb_logits"], p["dt_bias"], p["w_in"], p["w_dt"], p["layer"])
    ohg, yssm, *states = _mixer(
        heads5, zs, xbc, dt, hg0, ssm0, conv0, p["conv_w"], p["conv_b"], p["a_log"], p["d_skip"],
        p["hg_norm_w"], p["ssm_norm_w"], prev_out, layer=p["layer"], batch=batch, seq_len=seq_len)
    h = _outproj(h, ohg, yssm, gains[3:4], p["w_out"], p["layer"])
    h = _ffn(h, gains[4:6], *p["ffn2"], p["layer"])
    return h, states


def _trunk(x, s_hg, s_ssm, s_conv, layers):
    batch, seq_len, _ = x.shape
    h = x.reshape(batch * seq_len, D_MODEL)
    conv_pad = jnp.pad(s_conv, ((0, 0), (0, 0), (SUBLANES - (CONV_W - 1), 0), (0, 0)))
    states = None
    for p in layers:
        h, states = _layer(h, s_hg, s_ssm, conv_pad, states, p, batch=batch, seq_len=seq_len)
    hg_out, ssm_out, conv_out = states
    return h.reshape(batch, seq_len, D_MODEL), hg_out, ssm_out, conv_out[:, :, SUBLANES - (CONV_W - 1):, :]


def kernel(x_prompt, x_sample, state_hgrn, state_ssm, state_conv, w_in, hg_lb_logits, conv_w, conv_b, dt_bias, a_log, d_skip, hg_norm_w, ssm_norm_w, w_out, ffn1_w_gate, ffn1_w_up, ffn1_w_down, ffn2_w_gate, ffn2_w_up, ffn2_w_down, norm_gain):
    layers = _prep_layers(w_in, hg_lb_logits, conv_w, conv_b, dt_bias, a_log, d_skip, hg_norm_w, ssm_norm_w, w_out,
                          ffn1_w_gate, ffn1_w_up, ffn1_w_down, ffn2_w_gate, ffn2_w_up, ffn2_w_down, norm_gain)
    bp = x_prompt.shape[0]
    y_prompt, hg_p, ssm_p, conv_p = _trunk(
        x_prompt,
        jnp.zeros((DEPTH, bp, HG_HEADS, HG_KDIM, HG_VDIM), F32),
        jnp.zeros((DEPTH, bp, M_HEADS, M_HEADDIM, M_DSTATE), F32),
        jnp.zeros((DEPTH, bp, CONV_W - 1, CONV_DIM), F32),
        layers)
    y_sample, hg_s, ssm_s, conv_s = _trunk(x_sample, state_hgrn, state_ssm, state_conv, layers)
    return (y_prompt, y_sample, hg_p, ssm_p, conv_p, hg_s, ssm_s, conv_s)
```

```python
import functools

import jax
import jax.numpy as jnp
from jax import lax
from jax.experimental import pallas as pl
from jax.experimental.pallas import tpu as pltpu

F32 = jnp.float32
BF16 = jnp.bfloat16

D_MODEL = 1024
DEPTH = 2
HG_HEADS = 8
HG_KDIM = 128
HG_VDIM = 128
HG_WIDTH = HG_HEADS * HG_KDIM
LB_FLOOR = 1e-30
M_DINNER = 1024
M_HEADDIM = 64
M_HEADS = 16
M_DSTATE = 128
M_GROUPS = 2
M_GROUP_WIDTH = M_DINNER // M_GROUPS
CONV_W = 4
CONV_TAIL = CONV_W - 1
CONV_DIM = M_DINNER + 2 * M_GROUPS * M_DSTATE
D_FF = 2816
EPS = 1e-6

LANES = 128
SUBLANES = 8
DT_PAD = LANES
D_PROJ_MAIN = 4 * HG_WIDTH + M_DINNER + CONV_DIM
FFN_TILE = 512
OUTPROJ_TILE = 512
INPROJ_TILE = 512
INPROJ_COLS = 256
SAMPLE_SEQS_PER_STEP = 8
LOG2E = 1.4426950408889634
NEG_BIG = -1e30
VMEM_LIMIT = 56 * 1024 * 1024

_NT = (((1,), (1,)), ((), ()))
_TN = (((0,), (0,)), ((), ()))


def _rms(x, w):
    return x * lax.rsqrt(jnp.mean(x * x, axis=-1, keepdims=True) + EPS) * w


def _silu(x):
    return x * (1.0 / (1.0 + jnp.exp2(x * (-LOG2E))))


def _mm(a, b):
    return jnp.dot(a.astype(BF16), b.astype(BF16), preferred_element_type=F32)


def _mm_nt(a, b):
    return lax.dot_general(a.astype(BF16), b.astype(BF16), _NT, preferred_element_type=F32)


def _mm_tn(a, b):
    return lax.dot_general(a.astype(BF16), b.astype(BF16), _TN, preferred_element_type=F32)


def _resident(shape):
    nd = len(shape)
    return pl.BlockSpec(shape, lambda *_: (0,) * nd, pipeline_mode=pl.Buffered(1))


def _resident_layer(stacked, layer):
    return pl.BlockSpec((None,) + stacked.shape[1:], lambda *_: (layer, 0, 0), pipeline_mode=pl.Buffered(1))


def _token_tile(n_tokens, largest):
    for tm in (t for t in (1024, 512, 256, 128, 64, 32, 16, 8) if t <= largest):
        if n_tokens % tm == 0:
            return tm
    raise ValueError(f"token count {n_tokens} is not a multiple of 8")


FF_CHUNK = 256


def _ffn_body(x_ref, g_ref, wg_ref, wu_ref, wd_ref, o_ref, act_ref):
    x = x_ref[...]
    xn = _rms(x, g_ref[0:1, :]).astype(BF16)
    for j in range(D_FF // FF_CHUNK):
        sl = slice(j * FF_CHUNK, (j + 1) * FF_CHUNK)
        gate = jnp.dot(xn, wg_ref[:, sl], preferred_element_type=F32)
        up = jnp.dot(xn, wu_ref[:, sl], preferred_element_type=F32)
        act_ref[:, sl] = (_silu(gate) * up).astype(BF16)
    down = jnp.dot(act_ref[...], wd_ref[...], preferred_element_type=F32)
    o_ref[...] = x + 0.5 * _rms(down, g_ref[1:2, :])


def _ffn(x, gains2, wg, wu, wd, layer):
    n = x.shape[0]
    tm = _token_tile(n, largest=FFN_TILE)
    return pl.pallas_call(
        _ffn_body,
        out_shape=jax.ShapeDtypeStruct((n, D_MODEL), F32),
        grid=(n // tm,),
        in_specs=[
            pl.BlockSpec((tm, D_MODEL), lambda i: (i, 0)),
            _resident((2, D_MODEL)),
            _resident_layer(wg, layer),
            _resident_layer(wu, layer),
            _resident_layer(wd, layer),
        ],
        out_specs=pl.BlockSpec((tm, D_MODEL), lambda i: (i, 0)),
        scratch_shapes=[pltpu.VMEM((tm, D_FF), BF16)],
        compiler_params=pltpu.CompilerParams(dimension_semantics=("arbitrary",), vmem_limit_bytes=VMEM_LIMIT),
        name="ffn",
    )(x, gains2, wg, wu, wd)


W_PREP_ROWS = 1664


def _transpose_cast_body(wt_ref, o_ref):
    o_ref[...] = wt_ref[...].T.astype(BF16)


def _transpose_pad_body(wt_ref, o_ref):
    o_ref[...] = jnp.zeros(o_ref.shape, F32)
    o_ref[:, 0:M_HEADS] = wt_ref[...].T


def _prep_w_in(w_in):
    wt = jnp.swapaxes(w_in, 1, 2)
    w_dt = pl.pallas_call(
        _transpose_pad_body,
        out_shape=jax.ShapeDtypeStruct((DEPTH, D_MODEL, DT_PAD), F32),
        grid=(DEPTH,),
        in_specs=[pl.BlockSpec((None, M_HEADS, D_MODEL), lambda l: (l, D_PROJ_MAIN // M_HEADS, 0))],
        out_specs=pl.BlockSpec((None, D_MODEL, DT_PAD), lambda l: (l, 0, 0)),
        compiler_params=pltpu.CompilerParams(dimension_semantics=("arbitrary",)),
        name="w_dt_relayout",
    )(wt)
    w_main = pl.pallas_call(
        _transpose_cast_body,
        out_shape=jax.ShapeDtypeStruct((DEPTH, D_MODEL, D_PROJ_MAIN), BF16),
        grid=(DEPTH, D_PROJ_MAIN // W_PREP_ROWS),
        in_specs=[pl.BlockSpec((None, W_PREP_ROWS, D_MODEL), lambda l, j: (l, j, 0))],
        out_specs=pl.BlockSpec((None, D_MODEL, W_PREP_ROWS), lambda l, j: (l, 0, j)),
        compiler_params=pltpu.CompilerParams(dimension_semantics=("arbitrary", "arbitrary"),
                                             vmem_limit_bytes=VMEM_LIMIT),
        name="w_in_relayout",
    )(wt)
    return w_main, w_dt


_C_Q = 0
_C_F = HG_WIDTH
_C_I = 2 * HG_WIDTH
_C_G = 3 * HG_WIDTH
_C_Z = 4 * HG_WIDTH
_C_X = _C_Z + M_DINNER


def _inproj_body(layer, x_ref, g_ref, lbl_ref, dtb_ref, w_ref, wdt_ref,
                 q_ref, k_ref, gl_ref, v_ref, go_ref, z_ref, xbc_ref, dt_ref, xn_ref):
    xn_ref[...] = _rms(x_ref[...], g_ref[...]).astype(BF16)

    def proj(c0, width):
        return jnp.dot(xn_ref[...], w_ref[:, c0:c0 + width], preferred_element_type=F32)

    lg = lbl_ref[...]
    e = jnp.exp(lg - jnp.max(lg, axis=0, keepdims=True))
    p = e / jnp.sum(e, axis=0, keepdims=True)
    lb = jnp.sum(p[0:layer + 1, :], axis=0, keepdims=True) - p[0:1, :]
    lb_floor = jnp.maximum(lb, LB_FLOOR)

    def head_store(ref, c, val):
        for j in range(INPROJ_COLS // HG_KDIM):
            ref[c // HG_KDIM + j] = val[:, j * HG_KDIM:(j + 1) * HG_KDIM].astype(ref.dtype)

    def gate_block(c):
        cols = slice(c, c + INPROJ_COLS)
        fr = proj(_C_F + c, INPROJ_COLS)
        t = jnp.exp(-jnp.abs(fr))
        big = 1.0 / (1.0 + t)
        small = t * big
        positive = fr >= 0.0
        kk = (1.0 - lb[:, cols]) * jnp.where(positive, small, big)
        f_direct = lb_floor[:, cols] + (1.0 - lb[:, cols]) * jnp.where(positive, big, small)
        glog = jnp.where(positive, jnp.log1p((lb_floor[:, cols] - lb[:, cols]) - kk), jnp.log(f_direct))
        head_store(k_ref, c, kk)
        head_store(gl_ref, c, glog * LOG2E)

    def xbc_block(c):
        xbc_ref[:, c:c + INPROJ_COLS] = proj(_C_X + c, INPROJ_COLS)

    xbc_blocks = list(range(0, CONV_DIM, INPROJ_COLS))
    for c in range(0, HG_WIDTH, INPROJ_COLS):
        head_store(q_ref, c, proj(_C_Q + c, INPROJ_COLS) * (HG_KDIM ** -0.5))
        gate_block(c)
        xbc_block(xbc_blocks.pop())
        head_store(v_ref, c, proj(_C_I + c, INPROJ_COLS))
        head_store(go_ref, c, _silu(proj(_C_G + c, INPROJ_COLS)))
        z_ref[:, c:c + INPROJ_COLS] = _silu(proj(_C_Z + c, INPROJ_COLS)).astype(BF16)
    for c in xbc_blocks:
        xbc_block(c)
    dt_ref[...] = jax.nn.softplus(
        jnp.dot(xn_ref[...], wdt_ref[...].astype(BF16), preferred_element_type=F32) + dtb_ref[...])


def _inproj(h, gain, lb_logits, dt_bias_pad, w, w_dt, layer):
    n = h.shape[0]
    tm = _token_tile(n, largest=INPROJ_TILE)

    def heads(dtype):
        return jax.ShapeDtypeStruct((HG_HEADS, n, HG_KDIM), dtype)

    head_spec = pl.BlockSpec((HG_HEADS, tm, HG_KDIM), lambda i: (0, i, 0))

    def tok(width):
        return pl.BlockSpec((tm, width), lambda i: (i, 0))

    return pl.pallas_call(
        functools.partial(_inproj_body, layer),
        out_shape=(heads(F32), heads(F32), heads(F32), heads(BF16), heads(BF16),
                   jax.ShapeDtypeStruct((n, M_DINNER), BF16),
                   jax.ShapeDtypeStruct((n, CONV_DIM), F32),
                   jax.ShapeDtypeStruct((n, DT_PAD), F32)),
        grid=(n // tm,),
        in_specs=[
            tok(D_MODEL),
            _resident((1, D_MODEL)),
            _resident((DEPTH, HG_WIDTH)),
            _resident((1, DT_PAD)),
            _resident_layer(w, layer),
            _resident_layer(w_dt, layer),
        ],
        out_specs=(head_spec,) * 5 + (tok(M_DINNER), tok(CONV_DIM), tok(DT_PAD)),
        scratch_shapes=[pltpu.VMEM((tm, D_MODEL), BF16)],
        compiler_params=pltpu.CompilerParams(dimension_semantics=("arbitrary",), vmem_limit_bytes=VMEM_LIMIT),
        name="inproj",
    )(h, gain, lb_logits, dt_bias_pad, w, w_dt)


BAND_LONG = 4
BAND_SHORT = 2


def _expand_heads(vals, group):
    rows = vals.shape[0]
    lane = lax.broadcasted_iota(jnp.int32, (rows, LANES), 1)
    tiles = []
    for j in range(M_GROUP_WIDTH // LANES):
        h0 = group * (M_HEADS // M_GROUPS) + 2 * j
        lo = jnp.broadcast_to(vals[:, h0:h0 + 1], (rows, LANES))
        hi = jnp.broadcast_to(vals[:, h0 + 1:h0 + 2], (rows, LANES))
        tiles.append(jnp.where(lane < M_HEADDIM, lo, hi))
    return jnp.concatenate(tiles, axis=1)


def _as_column(row_vec):
    n = row_vec.shape[1]
    hi = row_vec.astype(BF16).astype(F32)
    mid = (row_vec - hi).astype(BF16).astype(F32)
    lo = row_vec - hi - mid
    sub = lax.broadcasted_iota(jnp.int32, (SUBLANES, n), 0)
    pieces = jnp.where(sub == 0, hi, jnp.where(sub == 1, mid, jnp.where(sub == 2, lo, 0.0)))
    return _mm_tn(pieces, jnp.ones((SUBLANES, n), F32))


_MIXER_INPUTS = 17


def _mixer_body(nseq, seq_chunk, steps, *refs):
    band = BAND_LONG if nseq == 1 else BAND_SHORT
    (q_ref, k_ref, gl_ref, v_ref, go_ref, z_ref, xbc_ref, dt_ref, hg0_ref, ssm0_ref, conv0_ref,
     convw_ref, convb_ref, alog_ref, dskip_ref, hgw_ref, ssmw_ref) = refs[:_MIXER_INPUTS]
    (ohg_ref, yssm_ref, hg_out_ref, ssm_out_ref, conv_out_ref,
     st_hg, st_ssm, cat_ref, shift_ref, mask_ref, class_ref, act_ref) = refs[-12:]
    chunk = nseq * seq_chunk
    step = pl.program_id(1)
    last_step = steps - 1
    heads_per_group = M_HEADS // M_GROUPS
    carried = steps > 1

    def when_step(which):
        if carried:
            return pl.when(step == which)
        return lambda block: block()

    def group_heads(g):
        return slice(g * heads_per_group, (g + 1) * heads_per_group)

    def read_hg(i, h):
        return st_hg[i, h] if carried else hg0_ref[i, h]

    def write_hg(i, h, val):
        if carried:
            st_hg[i, h] = val
        else:
            hg_out_ref[i, h] = val

    def read_ssm(i, g):
        return st_ssm[i, g] if carried else ssm0_ref[i, group_heads(g)].reshape(M_GROUP_WIDTH, M_DSTATE)

    def write_ssm(i, g, val):
        if carried:
            st_ssm[i, g] = val
        else:
            ssm_out_ref[i, group_heads(g)] = val.reshape(heads_per_group, M_HEADDIM, M_DSTATE)

    @when_step(0)
    def _load_states():
        for i in range(nseq):
            if carried:
                for h in range(HG_HEADS):
                    st_hg[i, h] = hg0_ref[i, h]
                for g in range(M_GROUPS):
                    st_ssm[i, g] = ssm0_ref[i, group_heads(g)].reshape(M_GROUP_WIDTH, M_DSTATE)
            for c in range(CONV_DIM // LANES):
                cat_ref[i, c, 0:SUBLANES - CONV_TAIL, :] = jnp.zeros((SUBLANES - CONV_TAIL, LANES), F32)
                cat_ref[i, c, SUBLANES - CONV_TAIL:SUBLANES, :] = conv0_ref[i, :, c * LANES:(c + 1) * LANES]
        shift_ref[:, :, 0:SUBLANES, :] = jnp.zeros((HG_HEADS, 2, SUBLANES, HG_KDIM), F32)

    levels = []
    m = band
    while 2 * m <= seq_chunk:
        levels.append(m)
        m *= 2

    @when_step(0)
    def _build_masks():
        row = lax.broadcasted_iota(jnp.int32, (chunk, chunk), 0)
        col = lax.broadcasted_iota(jnp.int32, (chunk, chunk), 1)
        causal = (col <= row) & ((row // seq_chunk) == (col // seq_chunk))
        mask_ref[...] = causal.astype(F32).astype(BF16)
        pair_class = jnp.full((chunk, chunk), -1, jnp.int32)
        for lvl in reversed(range(len(levels))):
            blk_log2 = jnp.int32((2 * levels[lvl]).bit_length() - 1)
            same_blk = lax.shift_right_logical(row, blk_log2) == lax.shift_right_logical(col, blk_log2)
            pair_class = jnp.where(same_blk, band + lvl, pair_class)
        pair_class = jnp.where(row - col < band, row - col, pair_class)
        class_ref[...] = jnp.where(causal, pair_class, -1)

    seq_rows = [slice(i * seq_chunk, (i + 1) * seq_chunk) for i in range(nseq)]

    def cumsum_rows(x):
        hi = x.astype(BF16)
        rest = x - hi.astype(F32)
        mid = rest.astype(BF16)
        lo = (rest - mid.astype(F32)).astype(BF16)
        return sum(jnp.dot(mask_ref[...], piece, preferred_element_type=F32) for piece in (hi, mid, lo))

    for h in range(HG_HEADS):
        shift_ref[h, 0, SUBLANES:SUBLANES + chunk, :] = cumsum_rows(gl_ref[h])
        shift_ref[h, 1, SUBLANES:SUBLANES + chunk, :] = k_ref[h]

    def gla_head(h):
        q = q_ref[h]
        k = k_ref[h]
        v = v_ref[h]
        b = shift_ref[h, 0, SUBLANES:SUBLANES + chunk, :]
        qe = q * jnp.exp2(b)
        o = jnp.concatenate([_mm(qe[r], read_hg(i, h)) for i, r in enumerate(seq_rows)], axis=0)

        scores = jnp.zeros((chunk, chunk), F32)
        for dist in range(band):
            if dist == 0:
                p = jnp.sum(q * k, axis=-1, keepdims=True)
            else:
                lo = SUBLANES - dist
                d = b - shift_ref[h, 0, lo:lo + chunk, :]
                if nseq > 1:
                    d = jnp.minimum(d, 0.0)
                p = jnp.sum(q * shift_ref[h, 1, lo:lo + chunk, :] * jnp.exp2(d), axis=-1, keepdims=True)
            scores = jnp.where(class_ref[...] == dist, p, scores)

        for lvl, m in enumerate(levels):
            blk = 2 * m
            bref = jnp.concatenate(
                [jnp.broadcast_to(b[j * blk + m - 1:j * blk + m, :], (blk, HG_KDIM)) for j in range(chunk // blk)],
                axis=0)
            e = jnp.exp2(-jnp.abs(b - bref))
            scores = jnp.where(class_ref[...] == band + lvl, _mm_nt(q * e, k * e), scores)
        o = o + _mm(scores, v)

        for i, r in enumerate(seq_rows):
            b_last = b[r.stop - 1:r.stop, :]
            kd = k[r] * jnp.exp2(b_last - b[r])
            write_hg(i, h, _as_column(jnp.exp2(b_last)) * read_hg(i, h) + _mm_tn(kd, v[r]))

        on = _rms(o, hgw_ref[h]) * go_ref[h]
        ohg_ref[:, h * HG_VDIM:(h + 1) * HG_VDIM] = on.astype(BF16)

    for i, r in enumerate(seq_rows):
        for c in range(CONV_DIM // LANES):
            lanes = slice(c * LANES, (c + 1) * LANES)
            cat_ref[i, c, SUBLANES:SUBLANES + seq_chunk, :] = xbc_ref[r, lanes]
            conv = convb_ref[:, lanes]
            for j in range(CONV_W):
                off = SUBLANES - (CONV_W - 1) + j
                conv = conv + cat_ref[i, c, off:off + seq_chunk, :] * convw_ref[j:j + 1, lanes]
            act_ref[r, lanes] = _silu(conv)
            cat_ref[i, c, 0:SUBLANES, :] = cat_ref[i, c, seq_chunk:seq_chunk + SUBLANES, :]

    lane_t = lax.broadcasted_iota(jnp.int32, (chunk, DT_PAD), 1)
    dt = jnp.where(lane_t < M_HEADS, dt_ref[...], 0.0)
    da = dt * (-LOG2E * jnp.exp(alog_ref[...]))
    cum = cumsum_rows(da)
    key_term = cum.T[0:M_HEADS, :] - jnp.log2(dt.T[0:M_HEADS, :])
    ecum = jnp.exp2(cum)
    lane_x = lax.broadcasted_iota(jnp.int32, (chunk, LANES), 1)
    pairs_per_group = M_GROUP_WIDTH // LANES
    assert HG_HEADS == M_GROUPS * pairs_per_group

    def group_operands(g):
        b_g = act_ref[:, M_DINNER + g * M_DSTATE:M_DINNER + (g + 1) * M_DSTATE]
        c_g = act_ref[:, M_DINNER + (M_GROUPS + g) * M_DSTATE:M_DINNER + (M_GROUPS + g + 1) * M_DSTATE]
        return b_g, c_g, act_ref[:, g * M_GROUP_WIDTH:(g + 1) * M_GROUP_WIDTH]

    cb_groups = [_mm_nt(group_operands(g)[1], group_operands(g)[0]) for g in range(M_GROUPS)]

    def ssd_pair(g, j):
        x_pair = act_ref[:, g * M_GROUP_WIDTH + j * LANES:g * M_GROUP_WIDTH + (j + 1) * LANES]
        causal = class_ref[...] >= 0
        y_pair = None
        for half in range(2):
            hd = g * heads_per_group + 2 * j + half
            diff = (jnp.broadcast_to(cum[:, hd:hd + 1], (chunk, chunk))
                    - jnp.broadcast_to(key_term[hd:hd + 1, :], (chunk, chunk)))
            mh = cb_groups[g] * jnp.exp2(jnp.where(causal, diff, NEG_BIG))
            keep = (lane_x < M_HEADDIM) if half == 0 else (lane_x >= M_HEADDIM)
            part = _mm(mh, jnp.where(keep, x_pair, 0.0))
            y_pair = part if y_pair is None else y_pair + part
        return y_pair

    y_tiles = [[None] * pairs_per_group for _ in range(M_GROUPS)]
    for h in range(HG_HEADS):
        gla_head(h)
        g, j = divmod(h, pairs_per_group)
        y_tiles[g][j] = ssd_pair(g, j)

    y_groups = []
    for g in range(M_GROUPS):
        b_g, c_g, x_g = group_operands(g)
        y_intra = jnp.concatenate(y_tiles[g], axis=1)
        y_inter = jnp.concatenate([_mm_nt(c_g[r], read_ssm(i, g)) for i, r in enumerate(seq_rows)], axis=0)
        y_g = y_intra + y_inter * _expand_heads(ecum, g)
        y_g = y_g + dskip_ref[:, g * M_GROUP_WIDTH:(g + 1) * M_GROUP_WIDTH] * x_g
        y_g = y_g * z_ref[:, g * M_GROUP_WIDTH:(g + 1) * M_GROUP_WIDTH]
        y_groups.append(_rms(y_g, ssmw_ref[:, g * M_GROUP_WIDTH:(g + 1) * M_GROUP_WIDTH]))

        for i, r in enumerate(seq_rows):
            cum_last = cum[r.stop - 1:r.stop, :]
            wgt = dt[r] * jnp.exp2(cum_last - cum[r])
            xw = x_g[r] * _expand_heads(wgt, g)
            decay = jnp.exp2(cum_last)
            decay_rows = jnp.concatenate(
                [jnp.broadcast_to(decay[:, hd:hd + 1], (M_HEADDIM, M_DSTATE))
                 for hd in range(g * heads_per_group, (g + 1) * heads_per_group)], axis=0)
            write_ssm(i, g, decay_rows * read_ssm(i, g) + _mm_tn(xw, b_g[r]))
    yssm_ref[...] = jnp.concatenate(y_groups, axis=1).astype(BF16)

    @when_step(last_step)
    def _store_states():
        for i in range(nseq):
            if carried:
                for h in range(HG_HEADS):
                    hg_out_ref[i, h] = st_hg[i, h]
                for g in range(M_GROUPS):
                    ssm_out_ref[i, group_heads(g)] = st_ssm[i, g].reshape(heads_per_group, M_HEADDIM, M_DSTATE)
            for c in range(CONV_DIM // LANES):
                conv_out_ref[i, :, c * LANES:(c + 1) * LANES] = cat_ref[i, c, SUBLANES - CONV_TAIL:SUBLANES, :]


def _mixer(heads5, zs, xbc, dt, hg0, ssm0, conv0, convw, convb, alog, dskip, hgw, ssmw, prev_out, *,
           layer, batch, seq_len):
    if seq_len >= LANES:
        nseq, seq_chunk = 1, LANES
    else:
        nseq, seq_chunk = SAMPLE_SEQS_PER_STEP, seq_len
    assert seq_len % seq_chunk == 0 and batch % nseq == 0 and seq_chunk % max(BAND_LONG, BAND_SHORT) == 0
    chunk = nseq * seq_chunk
    steps = seq_len // seq_chunk
    n = batch * seq_len

    def tok_idx(b, s):
        return b * steps + s

    head_spec = pl.BlockSpec((HG_HEADS, chunk, HG_KDIM), lambda b, s: (0, tok_idx(b, s), 0))

    def tok(width):
        return pl.BlockSpec((chunk, width), lambda b, s: (tok_idx(b, s), 0))

    hg_spec = pl.BlockSpec((None, nseq, HG_HEADS, HG_KDIM, HG_VDIM), lambda b, s: (layer, b, 0, 0, 0))
    ssm_spec = pl.BlockSpec((None, nseq, M_HEADS, M_HEADDIM, M_DSTATE), lambda b, s: (layer, b, 0, 0, 0))
    conv_spec = pl.BlockSpec((None, nseq, CONV_TAIL, CONV_DIM), lambda b, s: (layer, b, 0, 0))
    prev_out = () if prev_out is None else tuple(prev_out)
    first_state_out = 2
    assert len(prev_out) in (0, 3)
    aliases = {_MIXER_INPUTS + j: first_state_out + j for j in range(len(prev_out))}

    return pl.pallas_call(
        functools.partial(_mixer_body, nseq, seq_chunk, steps),
        out_shape=(
            jax.ShapeDtypeStruct((n, HG_WIDTH), BF16),
            jax.ShapeDtypeStruct((n, M_DINNER), BF16),
            jax.ShapeDtypeStruct(hg0.shape, F32),
            jax.ShapeDtypeStruct(ssm0.shape, F32),
            jax.ShapeDtypeStruct(conv0.shape, F32),
        ),
        grid=(batch // nseq, steps),
        in_specs=[head_spec] * 5 + [tok(M_DINNER), tok(CONV_DIM), tok(DT_PAD), hg_spec, ssm_spec, conv_spec,
                                    _resident((CONV_W, CONV_DIM)), _resident((1, CONV_DIM)), _resident((1, DT_PAD)),
                                    _resident((1, M_DINNER)), _resident((HG_HEADS, 1, HG_VDIM)), _resident((1, M_DINNER))]
        + [pl.BlockSpec(memory_space=pl.ANY)] * len(prev_out),
        out_specs=(tok(HG_WIDTH), tok(M_DINNER), hg_spec, ssm_spec, conv_spec),
        input_output_aliases=aliases,
        scratch_shapes=[
            pltpu.VMEM((nseq if steps > 1 else 0, HG_HEADS, HG_KDIM, HG_VDIM), F32),
            pltpu.VMEM((nseq if steps > 1 else 0, M_GROUPS, M_GROUP_WIDTH, M_DSTATE), F32),
            pltpu.VMEM((nseq, CONV_DIM // LANES, SUBLANES + seq_chunk, LANES), F32),
            pltpu.VMEM((HG_HEADS, 2, SUBLANES + chunk, HG_KDIM), F32),
            pltpu.VMEM((chunk, chunk), BF16),
            pltpu.VMEM((chunk, chunk), jnp.int32),
            pltpu.VMEM((chunk, CONV_DIM), F32),
        ],
        compiler_params=pltpu.CompilerParams(dimension_semantics=("arbitrary", "arbitrary"),
                                             vmem_limit_bytes=VMEM_LIMIT),
        name="mixer",
    )(*heads5, zs, xbc, dt, hg0, ssm0, conv0, convw, convb, alog, dskip, hgw, ssmw, *prev_out)


def _outproj_body(h_ref, ohg_ref, yssm_ref, g_ref, w_ref, o_ref):
    m = jnp.dot(ohg_ref[...], w_ref[0:HG_WIDTH, :], preferred_element_type=F32)
    m = m + jnp.dot(yssm_ref[...], w_ref[HG_WIDTH:HG_WIDTH + M_DINNER, :], preferred_element_type=F32)
    o_ref[...] = h_ref[...] + _rms(m, g_ref[...])


def _outproj(h, ohg, yssm, gain, w, layer):
    n = h.shape[0]
    tm = _token_tile(n, largest=OUTPROJ_TILE)

    def tok(width):
        return pl.BlockSpec((tm, width), lambda i: (i, 0))

    return pl.pallas_call(
        _outproj_body,
        out_shape=jax.ShapeDtypeStruct((n, D_MODEL), F32),
        grid=(n // tm,),
        in_specs=[tok(D_MODEL), tok(HG_WIDTH), tok(M_DINNER), _resident((1, D_MODEL)),
                  _resident_layer(w, layer)],
        out_specs=tok(D_MODEL),
        compiler_params=pltpu.CompilerParams(dimension_semantics=("arbitrary",), vmem_limit_bytes=VMEM_LIMIT),
        name="outproj",
    )(h, ohg, yssm, gain, w)


def _prep_layers(w_in, *rest):
    w_main, w_dt = _prep_w_in(w_in)
    return [_prep_layer(l, w_main, w_dt, *rest) for l in range(DEPTH)]


def _prep_layer(l, w_main, w_dt, hg_lb_logits, conv_w, conv_b, dt_bias, a_log, d_skip, hg_norm_w, ssm_norm_w, w_out,
                f1g, f1u, f1d, f2g, f2u, f2d, norm_gain):
    pad = DT_PAD - M_HEADS
    return dict(
        layer=l,
        w_in=w_main,
        w_dt=w_dt,
        lb_logits=hg_lb_logits,
        conv_w=conv_w[l],
        conv_b=conv_b[l][None, :],
        dt_bias=jnp.pad(dt_bias[l], (0, pad))[None, :],
        a_log=jnp.pad(a_log[l], (0, pad))[None, :],
        d_skip=jnp.repeat(d_skip[l], M_HEADDIM)[None, :],
        hg_norm_w=hg_norm_w[l].reshape(HG_HEADS, 1, HG_VDIM),
        ssm_norm_w=ssm_norm_w[l][None, :],
        w_out=w_out.astype(BF16),
        ffn1=(f1g.astype(BF16), f1u.astype(BF16), f1d.astype(BF16)),
        ffn2=(f2g.astype(BF16), f2u.astype(BF16), f2d.astype(BF16)),
        gains=norm_gain[l],
    )


def _layer(x, hg0, ssm0, conv0, prev_out, p, *, batch, seq_len):
    gains = p["gains"]
    h = _ffn(x, gains[0:2], *p["ffn1"], p["layer"])
    *heads5, zs, xbc, dt = _inproj(h, gains[2:3], p["lb_logits"], p["dt_bias"], p["w_in"], p["w_dt"], p["layer"])
    ohg, yssm, *states = _mixer(
        heads5, zs, xbc, dt, hg0, ssm0, conv0, p["conv_w"], p["conv_b"], p["a_log"], p["d_skip"],
        p["hg_norm_w"], p["ssm_norm_w"], prev_out, layer=p["layer"], batch=batch, seq_len=seq_len)
    h = _outproj(h, ohg, yssm, gains[3:4], p["w_out"], p["layer"])
    h = _ffn(h, gains[4:6], *p["ffn2"], p["layer"])
    return h, states


def _trunk(x, s_hg, s_ssm, s_conv, layers):
    batch, seq_len, _ = x.shape
    h = x.reshape(batch * seq_len, D_MODEL)
    states = None
    for p in layers:
        h, states = _layer(h, s_hg, s_ssm, s_conv, states, p, batch=batch, seq_len=seq_len)
    hg_out, ssm_out, conv_out = states
    return h.reshape(batch, seq_len, D_MODEL), hg_out, ssm_out, conv_out


def kernel(x_prompt, x_sample, state_hgrn, state_ssm, state_conv, w_in, hg_lb_logits, conv_w, conv_b, dt_bias, a_log, d_skip, hg_norm_w, ssm_norm_w, w_out, ffn1_w_gate, ffn1_w_up, ffn1_w_down, ffn2_w_gate, ffn2_w_up, ffn2_w_down, norm_gain):
    layers = _prep_layers(w_in, hg_lb_logits, conv_w, conv_b, dt_bias, a_log, d_skip, hg_norm_w, ssm_norm_w, w_out,
                          ffn1_w_gate, ffn1_w_up, ffn1_w_down, ffn2_w_gate, ffn2_w_up, ffn2_w_down, norm_gain)
    bp = x_prompt.shape[0]
    y_prompt, hg_p, ssm_p, conv_p = _trunk(
        x_prompt,
        jnp.zeros((DEPTH, bp, HG_HEADS, HG_KDIM, HG_VDIM), F32),
        jnp.zeros((DEPTH, bp, M_HEADS, M_HEADDIM, M_DSTATE), F32),
        jnp.zeros((DEPTH, bp, CONV_W - 1, CONV_DIM), F32),
        layers)
    y_sample, hg_s, ssm_s, conv_s = _trunk(x_sample, state_hgrn, state_ssm, state_conv, layers)
    return (y_prompt, y_sample, hg_p, ssm_p, conv_p, hg_s, ssm_s, conv_s)
```

```python
import functools

import jax
import jax.numpy as jnp
from jax import lax
from jax.experimental import pallas as pl
from jax.experimental.pallas import tpu as pltpu

F32 = jnp.float32
BF16 = jnp.bfloat16

D_MODEL = 1024
DEPTH = 2
HG_HEADS = 8
HG_KDIM = 128
HG_VDIM = 128
HG_WIDTH = HG_HEADS * HG_KDIM
LB_FLOOR = 1e-30
M_DINNER = 1024
M_HEADDIM = 64
M_HEADS = 16
M_DSTATE = 128
M_GROUPS = 2
M_GROUP_WIDTH = M_DINNER // M_GROUPS
CONV_W = 4
CONV_TAIL = CONV_W - 1
CONV_DIM = M_DINNER + 2 * M_GROUPS * M_DSTATE
D_FF = 2816
EPS = 1e-6

LANES = 128
SUBLANES = 8
DT_PAD = LANES
D_PROJ_MAIN = 4 * HG_WIDTH + M_DINNER + CONV_DIM
FFN_TILE = 512
OUTPROJ_TILE = 512
INPROJ_TILE = 512
INPROJ_COLS = 256
SAMPLE_SEQS_PER_STEP = 8
LOG2E = 1.4426950408889634
NEG_BIG = -1e30
VMEM_LIMIT = 56 * 1024 * 1024

_NT = (((1,), (1,)), ((), ()))
_TN = (((0,), (0,)), ((), ()))


def _rms(x, w):
    return x * lax.rsqrt(jnp.mean(x * x, axis=-1, keepdims=True) + EPS) * w


def _silu(x):
    return x * (1.0 / (1.0 + jnp.exp2(x * (-LOG2E))))


def _mm(a, b):
    return jnp.dot(a.astype(BF16), b.astype(BF16), preferred_element_type=F32)


def _mm_nt(a, b):
    return lax.dot_general(a.astype(BF16), b.astype(BF16), _NT, preferred_element_type=F32)


def _mm_tn(a, b):
    return lax.dot_general(a.astype(BF16), b.astype(BF16), _TN, preferred_element_type=F32)


def _resident(shape):
    nd = len(shape)
    return pl.BlockSpec(shape, lambda *_: (0,) * nd, pipeline_mode=pl.Buffered(1))


def _resident_layer(stacked, layer):
    return pl.BlockSpec((None,) + stacked.shape[1:], lambda *_: (layer, 0, 0), pipeline_mode=pl.Buffered(1))


def _token_tile(n_tokens, largest):
    for tm in (t for t in (1024, 512, 256, 128, 64, 32, 16, 8) if t <= largest):
        if n_tokens % tm == 0:
            return tm
    raise ValueError(f"token count {n_tokens} is not a multiple of 8")


FF_CHUNK = 256


def _ffn_body(first_tiles, n_in, n_out, *refs):
    x_refs = refs[:n_in]
    g_ref, wg_ref, wu_ref, wd_ref = refs[n_in:n_in + 4]
    o_refs = refs[n_in + 4:n_in + 4 + n_out]
    act_ref = refs[-1]
    in_first = pl.program_id(0) < first_tiles
    x = x_refs[0][...] if n_in == 1 else jnp.where(in_first, x_refs[0][...], x_refs[1][...])
    xn = _rms(x, g_ref[0:1, :]).astype(BF16)
    for j in range(D_FF // FF_CHUNK):
        sl = slice(j * FF_CHUNK, (j + 1) * FF_CHUNK)
        gate = jnp.dot(xn, wg_ref[:, sl], preferred_element_type=F32)
        up = jnp.dot(xn, wu_ref[:, sl], preferred_element_type=F32)
        act_ref[:, sl] = (_silu(gate) * up).astype(BF16)
    down = jnp.dot(act_ref[...], wd_ref[...], preferred_element_type=F32)
    out = x + 0.5 * _rms(down, g_ref[1:2, :])
    if n_out == 1:
        o_refs[0][...] = out
    else:
        @pl.when(in_first)
        def _():
            o_refs[0][...] = out

        @pl.when(jnp.logical_not(in_first))
        def _():
            o_refs[1][...] = out


def _ffn(xs, gains2, wg, wu, wd, layer, split_out=None):
    xs = tuple(xs) if isinstance(xs, (tuple, list)) else (xs,)
    n = sum(x.shape[0] for x in xs)
    out_rows = (n,) if split_out is None else tuple(split_out)
    first_rows = xs[0].shape[0] if len(xs) == 2 else out_rows[0]
    tm = _token_tile(first_rows, largest=FFN_TILE)
    assert all(r % tm == 0 for r in out_rows) and all(x.shape[0] % tm == 0 for x in xs) and sum(out_rows) == n
    first_tiles = first_rows // tm

    def specs(count):
        if count == 1:
            return [pl.BlockSpec((tm, D_MODEL), lambda i: (i, 0))]
        return [pl.BlockSpec((tm, D_MODEL), lambda i: (jnp.minimum(i, first_tiles - 1), 0)),
                pl.BlockSpec((tm, D_MODEL), lambda i: (jnp.maximum(i - first_tiles, 0), 0))]

    out = pl.pallas_call(
        functools.partial(_ffn_body, first_tiles, len(xs), len(out_rows)),
        out_shape=tuple(jax.ShapeDtypeStruct((r, D_MODEL), F32) for r in out_rows),
        grid=(n // tm,),
        in_specs=specs(len(xs)) + [
            _resident((2, D_MODEL)),
            _resident_layer(wg, layer),
            _resident_layer(wu, layer),
            _resident_layer(wd, layer),
        ],
        out_specs=tuple(specs(len(out_rows))),
        scratch_shapes=[pltpu.VMEM((tm, D_FF), BF16)],
        compiler_params=pltpu.CompilerParams(dimension_semantics=("arbitrary",), vmem_limit_bytes=VMEM_LIMIT),
        name="ffn",
    )(*xs, gains2, wg, wu, wd)
    return out[0] if split_out is None else out


W_PREP_ROWS = 1664


def _transpose_cast_body(wt_ref, o_ref):
    o_ref[...] = wt_ref[...].T.astype(BF16)


def _transpose_pad_body(wt_ref, o_ref):
    o_ref[...] = jnp.zeros(o_ref.shape, F32)
    o_ref[:, 0:M_HEADS] = wt_ref[...].T


def _prep_w_in(w_in):
    wt = jnp.swapaxes(w_in, 1, 2)
    w_dt = pl.pallas_call(
        _transpose_pad_body,
        out_shape=jax.ShapeDtypeStruct((DEPTH, D_MODEL, DT_PAD), F32),
        grid=(DEPTH,),
        in_specs=[pl.BlockSpec((None, M_HEADS, D_MODEL), lambda l: (l, D_PROJ_MAIN // M_HEADS, 0))],
        out_specs=pl.BlockSpec((None, D_MODEL, DT_PAD), lambda l: (l, 0, 0)),
        compiler_params=pltpu.CompilerParams(dimension_semantics=("arbitrary",)),
        name="w_dt_relayout",
    )(wt)
    w_main = pl.pallas_call(
        _transpose_cast_body,
        out_shape=jax.ShapeDtypeStruct((DEPTH, D_MODEL, D_PROJ_MAIN), BF16),
        grid=(DEPTH, D_PROJ_MAIN // W_PREP_ROWS),
        in_specs=[pl.BlockSpec((None, W_PREP_ROWS, D_MODEL), lambda l, j: (l, j, 0))],
        out_specs=pl.BlockSpec((None, D_MODEL, W_PREP_ROWS), lambda l, j: (l, 0, j)),
        compiler_params=pltpu.CompilerParams(dimension_semantics=("arbitrary", "arbitrary"),
                                             vmem_limit_bytes=VMEM_LIMIT),
        name="w_in_relayout",
    )(wt)
    return w_main, w_dt


_C_Q = 0
_C_F = HG_WIDTH
_C_I = 2 * HG_WIDTH
_C_G = 3 * HG_WIDTH
_C_Z = 4 * HG_WIDTH
_C_X = _C_Z + M_DINNER


def _inproj_body(layer, x_ref, g_ref, lbl_ref, dtb_ref, w_ref, wdt_ref,
                 q_ref, k_ref, gl_ref, v_ref, go_ref, z_ref, xbc_ref, dt_ref, xn_ref):
    xn_ref[...] = _rms(x_ref[...], g_ref[...]).astype(BF16)

    def proj(c0, width):
        return jnp.dot(xn_ref[...], w_ref[:, c0:c0 + width], preferred_element_type=F32)

    lg = lbl_ref[...]
    e = jnp.exp(lg - jnp.max(lg, axis=0, keepdims=True))
    p = e / jnp.sum(e, axis=0, keepdims=True)
    lb = jnp.sum(p[0:layer + 1, :], axis=0, keepdims=True) - p[0:1, :]
    lb_floor = jnp.maximum(lb, LB_FLOOR)

    def head_store(ref, c, val):
        for j in range(INPROJ_COLS // HG_KDIM):
            ref[c // HG_KDIM + j] = val[:, j * HG_KDIM:(j + 1) * HG_KDIM].astype(ref.dtype)

    def gate_block(c):
        cols = slice(c, c + INPROJ_COLS)
        fr = proj(_C_F + c, INPROJ_COLS)
        t = jnp.exp(-jnp.abs(fr))
        big = 1.0 / (1.0 + t)
        small = t * big
        positive = fr >= 0.0
        kk = (1.0 - lb[:, cols]) * jnp.where(positive, small, big)
        f_direct = lb_floor[:, cols] + (1.0 - lb[:, cols]) * jnp.where(positive, big, small)
        glog = jnp.where(positive, jnp.log1p((lb_floor[:, cols] - lb[:, cols]) - kk), jnp.log(f_direct))
        head_store(k_ref, c, kk)
        head_store(gl_ref, c, glog * LOG2E)

    def xbc_block(c):
        xbc_ref[:, c:c + INPROJ_COLS] = proj(_C_X + c, INPROJ_COLS)

    xbc_blocks = list(range(0, CONV_DIM, INPROJ_COLS))
    for c in range(0, HG_WIDTH, INPROJ_COLS):
        head_store(q_ref, c, proj(_C_Q + c, INPROJ_COLS) * (HG_KDIM ** -0.5))
        gate_block(c)
        xbc_block(xbc_blocks.pop())
        head_store(v_ref, c, proj(_C_I + c, INPROJ_COLS))
        head_store(go_ref, c, _silu(proj(_C_G + c, INPROJ_COLS)))
        z_ref[:, c:c + INPROJ_COLS] = _silu(proj(_C_Z + c, INPROJ_COLS)).astype(BF16)
    for c in xbc_blocks:
        xbc_block(c)
    dt_ref[...] = jax.nn.softplus(
        jnp.dot(xn_ref[...], wdt_ref[...].astype(BF16), preferred_element_type=F32) + dtb_ref[...])


def _inproj(h, gain, lb_logits, dt_bias_pad, w, w_dt, layer):
    n = h.shape[0]
    tm = _token_tile(n, largest=INPROJ_TILE)

    def heads(dtype):
        return jax.ShapeDtypeStruct((HG_HEADS, n, HG_KDIM), dtype)

    head_spec = pl.BlockSpec((HG_HEADS, tm, HG_KDIM), lambda i: (0, i, 0))

    def tok(width):
        return pl.BlockSpec((tm, width), lambda i: (i, 0))

    return pl.pallas_call(
        functools.partial(_inproj_body, layer),
        out_shape=(heads(F32), heads(F32), heads(F32), heads(BF16), heads(BF16),
                   jax.ShapeDtypeStruct((n, M_DINNER), BF16),
                   jax.ShapeDtypeStruct((n, CONV_DIM), F32),
                   jax.ShapeDtypeStruct((n, DT_PAD), F32)),
        grid=(n // tm,),
        in_specs=[
            tok(D_MODEL),
            _resident((1, D_MODEL)),
            _resident((DEPTH, HG_WIDTH)),
            _resident((1, DT_PAD)),
            _resident_layer(w, layer),
            _resident_layer(w_dt, layer),
        ],
        out_specs=(head_spec,) * 5 + (tok(M_DINNER), tok(CONV_DIM), tok(DT_PAD)),
        scratch_shapes=[pltpu.VMEM((tm, D_MODEL), BF16)],
        compiler_params=pltpu.CompilerParams(dimension_semantics=("arbitrary",), vmem_limit_bytes=VMEM_LIMIT),
        name="inproj",
    )(h, gain, lb_logits, dt_bias_pad, w, w_dt)


BAND_LONG = 4
BAND_SHORT = 2


def _expand_heads(vals, group):
    rows = vals.shape[0]
    lane = lax.broadcasted_iota(jnp.int32, (rows, LANES), 1)
    tiles = []
    for j in range(M_GROUP_WIDTH // LANES):
        h0 = group * (M_HEADS // M_GROUPS) + 2 * j
        lo = jnp.broadcast_to(vals[:, h0:h0 + 1], (rows, LANES))
        hi = jnp.broadcast_to(vals[:, h0 + 1:h0 + 2], (rows, LANES))
        tiles.append(jnp.where(lane < M_HEADDIM, lo, hi))
    return jnp.concatenate(tiles, axis=1)


def _as_column(row_vec):
    n = row_vec.shape[1]
    hi = row_vec.astype(BF16).astype(F32)
    mid = (row_vec - hi).astype(BF16).astype(F32)
    lo = row_vec - hi - mid
    sub = lax.broadcasted_iota(jnp.int32, (SUBLANES, n), 0)
    pieces = jnp.where(sub == 0, hi, jnp.where(sub == 1, mid, jnp.where(sub == 2, lo, 0.0)))
    return _mm_tn(pieces, jnp.ones((SUBLANES, n), F32))


_MIXER_INPUTS = 17


def _mixer_body(nseq, seq_chunk, steps, *refs):
    band = BAND_LONG if nseq == 1 else BAND_SHORT
    (q_ref, k_ref, gl_ref, v_ref, go_ref, z_ref, xbc_ref, dt_ref, hg0_ref, ssm0_ref, conv0_ref,
     convw_ref, convb_ref, alog_ref, dskip_ref, hgw_ref, ssmw_ref) = refs[:_MIXER_INPUTS]
    (ohg_ref, yssm_ref, hg_out_ref, ssm_out_ref, conv_out_ref,
     st_hg, st_ssm, cat_ref, shift_ref, mask_ref, class_ref, act_ref) = refs[-12:]
    chunk = nseq * seq_chunk
    step = pl.program_id(1)
    last_step = steps - 1
    heads_per_group = M_HEADS // M_GROUPS
    carried = steps > 1

    def when_step(which):
        if carried:
            return pl.when(step == which)
        return lambda block: block()

    def group_heads(g):
        return slice(g * heads_per_group, (g + 1) * heads_per_group)

    def read_hg(i, h):
        return st_hg[i, h] if carried else hg0_ref[i, h]

    def write_hg(i, h, val):
        if carried:
            st_hg[i, h] = val
        else:
            hg_out_ref[i, h] = val

    def read_ssm(i, g):
        return st_ssm[i, g] if carried else ssm0_ref[i, group_heads(g)].reshape(M_GROUP_WIDTH, M_DSTATE)

    def write_ssm(i, g, val):
        if carried:
            st_ssm[i, g] = val
        else:
            ssm_out_ref[i, group_heads(g)] = val.reshape(heads_per_group, M_HEADDIM, M_DSTATE)

    @when_step(0)
    def _load_states():
        for i in range(nseq):
            if carried:
                for h in range(HG_HEADS):
                    st_hg[i, h] = hg0_ref[i, h]
                for g in range(M_GROUPS):
                    st_ssm[i, g] = ssm0_ref[i, group_heads(g)].reshape(M_GROUP_WIDTH, M_DSTATE)
            for c in range(CONV_DIM // LANES):
                cat_ref[i, c, 0:SUBLANES - CONV_TAIL, :] = jnp.zeros((SUBLANES - CONV_TAIL, LANES), F32)
                cat_ref[i, c, SUBLANES - CONV_TAIL:SUBLANES, :] = conv0_ref[i, :, c * LANES:(c + 1) * LANES]
        shift_ref[:, :, 0:SUBLANES, :] = jnp.zeros((HG_HEADS, 2, SUBLANES, HG_KDIM), F32)

    levels = []
    m = band
    while 2 * m <= seq_chunk:
        levels.append(m)
        m *= 2

    @when_step(0)
    def _build_masks():
        row = lax.broadcasted_iota(jnp.int32, (chunk, chunk), 0)
        col = lax.broadcasted_iota(jnp.int32, (chunk, chunk), 1)
        causal = (col <= row) & ((row // seq_chunk) == (col // seq_chunk))
        mask_ref[...] = causal.astype(F32).astype(BF16)
        pair_class = jnp.full((chunk, chunk), -1, jnp.int32)
        for lvl in reversed(range(len(levels))):
            blk_log2 = jnp.int32((2 * levels[lvl]).bit_length() - 1)
            same_blk = lax.shift_right_logical(row, blk_log2) == lax.shift_right_logical(col, blk_log2)
            pair_class = jnp.where(same_blk, band + lvl, pair_class)
        pair_class = jnp.where(row - col < band, row - col, pair_class)
        class_ref[...] = jnp.where(causal, pair_class, -1)

    seq_rows = [slice(i * seq_chunk, (i + 1) * seq_chunk) for i in range(nseq)]

    def cumsum_rows(x):
        hi = x.astype(BF16)
        rest = x - hi.astype(F32)
        mid = rest.astype(BF16)
        lo = (rest - mid.astype(F32)).astype(BF16)
        return sum(jnp.dot(mask_ref[...], piece, preferred_element_type=F32) for piece in (hi, mid, lo))

    for h in range(HG_HEADS):
        shift_ref[h, 0, SUBLANES:SUBLANES + chunk, :] = cumsum_rows(gl_ref[h])
        shift_ref[h, 1, SUBLANES:SUBLANES + chunk, :] = k_ref[h]

    def gla_head(h):
        q = q_ref[h]
        k = k_ref[h]
        v = v_ref[h]
        b = shift_ref[h, 0, SUBLANES:SUBLANES + chunk, :]
        qe = q * jnp.exp2(b)
        o = jnp.concatenate([_mm(qe[r], read_hg(i, h)) for i, r in enumerate(seq_rows)], axis=0)

        scores = jnp.zeros((chunk, chunk), F32)
        for dist in range(band):
            if dist == 0:
                p = jnp.sum(q * k, axis=-1, keepdims=True)
            else:
                lo = SUBLANES - dist
                d = b - shift_ref[h, 0, lo:lo + chunk, :]
                if nseq > 1:
                    d = jnp.minimum(d, 0.0)
                p = jnp.sum(q * shift_ref[h, 1, lo:lo + chunk, :] * jnp.exp2(d), axis=-1, keepdims=True)
            scores = jnp.where(class_ref[...] == dist, p, scores)

        for lvl, m in enumerate(levels):
            blk = 2 * m
            bref = jnp.concatenate(
                [jnp.broadcast_to(b[j * blk + m - 1:j * blk + m, :], (blk, HG_KDIM)) for j in range(chunk // blk)],
                axis=0)
            e = jnp.exp2(-jnp.abs(b - bref))
            scores = jnp.where(class_ref[...] == band + lvl, _mm_nt(q * e, k * e), scores)
        o = o + _mm(scores, v)

        for i, r in enumerate(seq_rows):
            b_last = b[r.stop - 1:r.stop, :]
            kd = k[r] * jnp.exp2(b_last - b[r])
            write_hg(i, h, _as_column(jnp.exp2(b_last)) * read_hg(i, h) + _mm_tn(kd, v[r]))

        on = _rms(o, hgw_ref[h]) * go_ref[h]
        ohg_ref[:, h * HG_VDIM:(h + 1) * HG_VDIM] = on.astype(BF16)

    for i, r in enumerate(seq_rows):
        for c in range(CONV_DIM // LANES):
            lanes = slice(c * LANES, (c + 1) * LANES)
            cat_ref[i, c, SUBLANES:SUBLANES + seq_chunk, :] = xbc_ref[r, lanes]
            conv = convb_ref[:, lanes]
            for j in range(CONV_W):
                off = SUBLANES - (CONV_W - 1) + j
                conv = conv + cat_ref[i, c, off:off + seq_chunk, :] * convw_ref[j:j + 1, lanes]
            act_ref[r, lanes] = _silu(conv)
            cat_ref[i, c, 0:SUBLANES, :] = cat_ref[i, c, seq_chunk:seq_chunk + SUBLANES, :]

    lane_t = lax.broadcasted_iota(jnp.int32, (chunk, DT_PAD), 1)
    dt = jnp.where(lane_t < M_HEADS, dt_ref[...], 0.0)
    da = dt * (-LOG2E * jnp.exp(alog_ref[...]))
    cum = cumsum_rows(da)
    key_term = cum.T[0:M_HEADS, :] - jnp.log2(dt.T[0:M_HEADS, :])
    ecum = jnp.exp2(cum)
    lane_x = lax.broadcasted_iota(jnp.int32, (chunk, LANES), 1)
    pairs_per_group = M_GROUP_WIDTH // LANES
    assert HG_HEADS == M_GROUPS * pairs_per_group

    def group_operands(g):
        b_g = act_ref[:, M_DINNER + g * M_DSTATE:M_DINNER + (g + 1) * M_DSTATE]
        c_g = act_ref[:, M_DINNER + (M_GROUPS + g) * M_DSTATE:M_DINNER + (M_GROUPS + g + 1) * M_DSTATE]
        return b_g, c_g, act_ref[:, g * M_GROUP_WIDTH:(g + 1) * M_GROUP_WIDTH]

    cb_groups = [_mm_nt(group_operands(g)[1], group_operands(g)[0]) for g in range(M_GROUPS)]

    def ssd_pair(g, j):
        x_pair = act_ref[:, g * M_GROUP_WIDTH + j * LANES:g * M_GROUP_WIDTH + (j + 1) * LANES]
        causal = class_ref[...] >= 0
        y_pair = None
        for half in range(2):
            hd = g * heads_per_group + 2 * j + half
            diff = (jnp.broadcast_to(cum[:, hd:hd + 1], (chunk, chunk))
                    - jnp.broadcast_to(key_term[hd:hd + 1, :], (chunk, chunk)))
            mh = cb_groups[g] * jnp.exp2(jnp.where(causal, diff, NEG_BIG))
            keep = (lane_x < M_HEADDIM) if half == 0 else (lane_x >= M_HEADDIM)
            part = _mm(mh, jnp.where(keep, x_pair, 0.0))
            y_pair = part if y_pair is None else y_pair + part
        return y_pair

    y_tiles = [[None] * pairs_per_group for _ in range(M_GROUPS)]
    for h in range(HG_HEADS):
        gla_head(h)
        g, j = divmod(h, pairs_per_group)
        y_tiles[g][j] = ssd_pair(g, j)

    y_groups = []
    for g in range(M_GROUPS):
        b_g, c_g, x_g = group_operands(g)
        y_intra = jnp.concatenate(y_tiles[g], axis=1)
        y_inter = jnp.concatenate([_mm_nt(c_g[r], read_ssm(i, g)) for i, r in enumerate(seq_rows)], axis=0)
        y_g = y_intra + y_inter * _expand_heads(ecum, g)
        y_g = y_g + dskip_ref[:, g * M_GROUP_WIDTH:(g + 1) * M_GROUP_WIDTH] * x_g
        y_g = y_g * z_ref[:, g * M_GROUP_WIDTH:(g + 1) * M_GROUP_WIDTH]
        y_groups.append(_rms(y_g, ssmw_ref[:, g * M_GROUP_WIDTH:(g + 1) * M_GROUP_WIDTH]))

        for i, r in enumerate(seq_rows):
            cum_last = cum[r.stop - 1:r.stop, :]
            wgt = dt[r] * jnp.exp2(cum_last - cum[r])
            xw = x_g[r] * _expand_heads(wgt, g)
            decay = jnp.exp2(cum_last)
            decay_rows = jnp.concatenate(
                [jnp.broadcast_to(decay[:, hd:hd + 1], (M_HEADDIM, M_DSTATE))
                 for hd in range(g * heads_per_group, (g + 1) * heads_per_group)], axis=0)
            write_ssm(i, g, decay_rows * read_ssm(i, g) + _mm_tn(xw, b_g[r]))
    yssm_ref[...] = jnp.concatenate(y_groups, axis=1).astype(BF16)

    @when_step(last_step)
    def _store_states():
        for i in range(nseq):
            if carried:
                for h in range(HG_HEADS):
                    hg_out_ref[i, h] = st_hg[i, h]
                for g in range(M_GROUPS):
                    ssm_out_ref[i, group_heads(g)] = st_ssm[i, g].reshape(heads_per_group, M_HEADDIM, M_DSTATE)
            for c in range(CONV_DIM // LANES):
                conv_out_ref[i, :, c * LANES:(c + 1) * LANES] = cat_ref[i, c, SUBLANES - CONV_TAIL:SUBLANES, :]


def _mixer(heads5, zs, xbc, dt, hg0, ssm0, conv0, convw, convb, alog, dskip, hgw, ssmw, prev_out, tok_out, *,
           layer, batch, seq_len, tok_offset):
    if seq_len >= LANES:
        nseq, seq_chunk = 1, LANES
    else:
        nseq, seq_chunk = SAMPLE_SEQS_PER_STEP, seq_len
    assert seq_len % seq_chunk == 0 and batch % nseq == 0 and seq_chunk % max(BAND_LONG, BAND_SHORT) == 0
    chunk = nseq * seq_chunk
    steps = seq_len // seq_chunk
    n = zs.shape[0]
    assert tok_offset % chunk == 0 and tok_offset + batch * seq_len <= n

    def tok_idx(b, s):
        return tok_offset // chunk + b * steps + s

    head_spec = pl.BlockSpec((HG_HEADS, chunk, HG_KDIM), lambda b, s: (0, tok_idx(b, s), 0))

    def tok(width):
        return pl.BlockSpec((chunk, width), lambda b, s: (tok_idx(b, s), 0))

    hg_spec = pl.BlockSpec((None, nseq, HG_HEADS, HG_KDIM, HG_VDIM), lambda b, s: (layer, b, 0, 0, 0))
    ssm_spec = pl.BlockSpec((None, nseq, M_HEADS, M_HEADDIM, M_DSTATE), lambda b, s: (layer, b, 0, 0, 0))
    conv_spec = pl.BlockSpec((None, nseq, CONV_TAIL, CONV_DIM), lambda b, s: (layer, b, 0, 0))
    prev_out = () if prev_out is None else tuple(prev_out)
    tok_out = () if tok_out is None else tuple(tok_out)
    first_state_out = 2
    assert len(prev_out) in (0, 3) and len(tok_out) in (0, first_state_out)
    aliases = {_MIXER_INPUTS + j: first_state_out + j for j in range(len(prev_out))}
    aliases.update({_MIXER_INPUTS + len(prev_out) + j: j for j in range(len(tok_out))})

    return pl.pallas_call(
        functools.partial(_mixer_body, nseq, seq_chunk, steps),
        out_shape=(
            jax.ShapeDtypeStruct((n, HG_WIDTH), BF16),
            jax.ShapeDtypeStruct((n, M_DINNER), BF16),
            jax.ShapeDtypeStruct(hg0.shape, F32),
            jax.ShapeDtypeStruct(ssm0.shape, F32),
            jax.ShapeDtypeStruct(conv0.shape, F32),
        ),
        grid=(batch // nseq, steps),
        in_specs=[head_spec] * 5 + [tok(M_DINNER), tok(CONV_DIM), tok(DT_PAD), hg_spec, ssm_spec, conv_spec,
                                    _resident((CONV_W, CONV_DIM)), _resident((1, CONV_DIM)), _resident((1, DT_PAD)),
                                    _resident((1, M_DINNER)), _resident((HG_HEADS, 1, HG_VDIM)), _resident((1, M_DINNER))]
        + [pl.BlockSpec(memory_space=pl.ANY)] * (len(prev_out) + len(tok_out)),
        out_specs=(tok(HG_WIDTH), tok(M_DINNER), hg_spec, ssm_spec, conv_spec),
        input_output_aliases=aliases,
        scratch_shapes=[
            pltpu.VMEM((nseq if steps > 1 else 0, HG_HEADS, HG_KDIM, HG_VDIM), F32),
            pltpu.VMEM((nseq if steps > 1 else 0, M_GROUPS, M_GROUP_WIDTH, M_DSTATE), F32),
            pltpu.VMEM((nseq, CONV_DIM // LANES, SUBLANES + seq_chunk, LANES), F32),
            pltpu.VMEM((HG_HEADS, 2, SUBLANES + chunk, HG_KDIM), F32),
            pltpu.VMEM((chunk, chunk), BF16),
            pltpu.VMEM((chunk, chunk), jnp.int32),
            pltpu.VMEM((chunk, CONV_DIM), F32),
        ],
        compiler_params=pltpu.CompilerParams(dimension_semantics=("arbitrary", "arbitrary"),
                                             vmem_limit_bytes=VMEM_LIMIT),
        name="mixer",
    )(*heads5, zs, xbc, dt, hg0, ssm0, conv0, convw, convb, alog, dskip, hgw, ssmw, *prev_out, *tok_out)


def _outproj_body(h_ref, ohg_ref, yssm_ref, g_ref, w_ref, o_ref):
    m = jnp.dot(ohg_ref[...], w_ref[0:HG_WIDTH, :], preferred_element_type=F32)
    m = m + jnp.dot(yssm_ref[...], w_ref[HG_WIDTH:HG_WIDTH + M_DINNER, :], preferred_element_type=F32)
    o_ref[...] = h_ref[...] + _rms(m, g_ref[...])


def _outproj(h, ohg, yssm, gain, w, layer):
    n = h.shape[0]
    tm = _token_tile(n, largest=OUTPROJ_TILE)

    def tok(width):
        return pl.BlockSpec((tm, width), lambda i: (i, 0))

    return pl.pallas_call(
        _outproj_body,
        out_shape=jax.ShapeDtypeStruct((n, D_MODEL), F32),
        grid=(n // tm,),
        in_specs=[tok(D_MODEL), tok(HG_WIDTH), tok(M_DINNER), _resident((1, D_MODEL)),
                  _resident_layer(w, layer)],
        out_specs=tok(D_MODEL),
        compiler_params=pltpu.CompilerParams(dimension_semantics=("arbitrary",), vmem_limit_bytes=VMEM_LIMIT),
        name="outproj",
    )(h, ohg, yssm, gain, w)


def _prep_layers(w_in, *rest):
    w_main, w_dt = _prep_w_in(w_in)
    return [_prep_layer(l, w_main, w_dt, *rest) for l in range(DEPTH)]


def _prep_layer(l, w_main, w_dt, hg_lb_logits, conv_w, conv_b, dt_bias, a_log, d_skip, hg_norm_w, ssm_norm_w, w_out,
                f1g, f1u, f1d, f2g, f2u, f2d, norm_gain):
    pad = DT_PAD - M_HEADS
    return dict(
        layer=l,
        w_in=w_main,
        w_dt=w_dt,
        lb_logits=hg_lb_logits,
        conv_w=conv_w[l],
        conv_b=conv_b[l][None, :],
        dt_bias=jnp.pad(dt_bias[l], (0, pad))[None, :],
        a_log=jnp.pad(a_log[l], (0, pad))[None, :],
        d_skip=jnp.repeat(d_skip[l], M_HEADDIM)[None, :],
        hg_norm_w=hg_norm_w[l].reshape(HG_HEADS, 1, HG_VDIM),
        ssm_norm_w=ssm_norm_w[l][None, :],
        w_out=w_out.astype(BF16),
        ffn1=(f1g.astype(BF16), f1u.astype(BF16), f1d.astype(BF16)),
        ffn2=(f2g.astype(BF16), f2u.astype(BF16), f2d.astype(BF16)),
        gains=norm_gain[l],
    )


def _trunks(xs, states0, layers):
    shapes = [x.shape[:2] for x in xs]
    rows = [b * l for b, l in shapes]
    offsets = [sum(rows[:g]) for g in range(len(xs))]
    h = tuple(x.reshape(r, D_MODEL) for x, r in zip(xs, rows))
    states = [None] * len(xs)
    for p in layers:
        gains, l = p["gains"], p["layer"]
        h = _ffn(h, gains[0:2], *p["ffn1"], l)
        *heads5, zs, xbc, dt = _inproj(h, gains[2:3], p["lb_logits"], p["dt_bias"], p["w_in"], p["w_dt"], l)
        tok_out = None
        for g, (batch, seq_len) in enumerate(shapes):
            ohg, yssm, *states[g] = _mixer(
                heads5, zs, xbc, dt, *states0[g], p["conv_w"], p["conv_b"], p["a_log"], p["d_skip"],
                p["hg_norm_w"], p["ssm_norm_w"], states[g], tok_out,
                layer=l, batch=batch, seq_len=seq_len, tok_offset=offsets[g])
            tok_out = (ohg, yssm)
        h = _outproj(h, ohg, yssm, gains[3:4], p["w_out"], l)
        h = _ffn(h, gains[4:6], *p["ffn2"], l, split_out=rows if l == DEPTH - 1 else None)
    ys = tuple(y.reshape(b, s, D_MODEL) for y, (b, s) in zip(h, shapes))
    return ys, states


def kernel(x_prompt, x_sample, state_hgrn, state_ssm, state_conv, w_in, hg_lb_logits, conv_w, conv_b, dt_bias, a_log, d_skip, hg_norm_w, ssm_norm_w, w_out, ffn1_w_gate, ffn1_w_up, ffn1_w_down, ffn2_w_gate, ffn2_w_up, ffn2_w_down, norm_gain):
    layers = _prep_layers(w_in, hg_lb_logits, conv_w, conv_b, dt_bias, a_log, d_skip, hg_norm_w, ssm_norm_w, w_out,
                          ffn1_w_gate, ffn1_w_up, ffn1_w_down, ffn2_w_gate, ffn2_w_up, ffn2_w_down, norm_gain)
    bp = x_prompt.shape[0]
    zero_states = (jnp.zeros((DEPTH, bp, HG_HEADS, HG_KDIM, HG_VDIM), F32),
                   jnp.zeros((DEPTH, bp, M_HEADS, M_HEADDIM, M_DSTATE), F32),
                   jnp.zeros((DEPTH, bp, CONV_W - 1, CONV_DIM), F32))
    (y_prompt, y_sample), (states_p, states_s) = _trunks(
        (x_prompt, x_sample), (zero_states, (state_hgrn, state_ssm, state_conv)), layers)
    return (y_prompt, y_sample, *states_p, *states_s)
```

```python
import functools

import jax
import jax.numpy as jnp
from jax import lax
from jax.experimental import pallas as pl
from jax.experimental.pallas import tpu as pltpu

F32 = jnp.float32
BF16 = jnp.bfloat16

D_MODEL = 1024
DEPTH = 2
HG_HEADS = 8
HG_KDIM = 128
HG_VDIM = 128
HG_WIDTH = HG_HEADS * HG_KDIM
LB_FLOOR = 1e-30
M_DINNER = 1024
M_HEADDIM = 64
M_HEADS = 16
M_DSTATE = 128
M_GROUPS = 2
M_GROUP_WIDTH = M_DINNER // M_GROUPS
CONV_W = 4
CONV_TAIL = CONV_W - 1
CONV_DIM = M_DINNER + 2 * M_GROUPS * M_DSTATE
D_FF = 2816
EPS = 1e-6

LANES = 128
SUBLANES = 8
DT_PAD = LANES
D_PROJ_MAIN = 4 * HG_WIDTH + M_DINNER + CONV_DIM
FFN_TILE = 1024
OUTPROJ_TILE = 1024
INPROJ_TILE = 512
INPROJ_COLS = 256
SAMPLE_SEQS_PER_STEP = 8
LOG2E = 1.4426950408889634
NEG_BIG = -1e30
VMEM_LIMIT = 56 * 1024 * 1024

_NT = (((1,), (1,)), ((), ()))
_TN = (((0,), (0,)), ((), ()))


def _rms(x, w):
    return x * lax.rsqrt(jnp.mean(x * x, axis=-1, keepdims=True) + EPS) * w


def _silu(x):
    return x * (1.0 / (1.0 + jnp.exp2(x * (-LOG2E))))


def _mm(a, b):
    return jnp.dot(a.astype(BF16), b.astype(BF16), preferred_element_type=F32)


def _mm_nt(a, b):
    return lax.dot_general(a.astype(BF16), b.astype(BF16), _NT, preferred_element_type=F32)


def _mm_tn(a, b):
    return lax.dot_general(a.astype(BF16), b.astype(BF16), _TN, preferred_element_type=F32)


def _resident(shape):
    nd = len(shape)
    return pl.BlockSpec(shape, lambda *_: (0,) * nd, pipeline_mode=pl.Buffered(1))


def _resident_layer(stacked, layer):
    return pl.BlockSpec((None,) + stacked.shape[1:], lambda *_: (layer, 0, 0), pipeline_mode=pl.Buffered(1))


def _token_tile(n_tokens, largest):
    for tm in (t for t in (1024, 512, 256, 128, 64, 32, 16, 8) if t <= largest):
        if n_tokens % tm == 0:
            return tm
    raise ValueError(f"token count {n_tokens} is not a multiple of 8")


FF_CHUNK = 256


def _ffn_body(first_tiles, n_in, n_out, *refs):
    x_refs = refs[:n_in]
    g_ref, wg_ref, wu_ref, wd_ref = refs[n_in:n_in + 4]
    o_refs = refs[n_in + 4:n_in + 4 + n_out]
    act_ref = refs[-1]
    in_first = pl.program_id(0) < first_tiles
    x = x_refs[0][...] if n_in == 1 else jnp.where(in_first, x_refs[0][...], x_refs[1][...])
    xn = _rms(x, g_ref[0:1, :]).astype(BF16)
    for j in range(D_FF // FF_CHUNK):
        sl = slice(j * FF_CHUNK, (j + 1) * FF_CHUNK)
        gate = jnp.dot(xn, wg_ref[:, sl], preferred_element_type=F32)
        up = jnp.dot(xn, wu_ref[:, sl], preferred_element_type=F32)
        act_ref[:, sl] = (_silu(gate) * up).astype(BF16)
    down = jnp.dot(act_ref[...], wd_ref[...], preferred_element_type=F32)
    out = x + 0.5 * _rms(down, g_ref[1:2, :])
    if n_out == 1:
        o_refs[0][...] = out
    else:
        @pl.when(in_first)
        def _():
            o_refs[0][...] = out

        @pl.when(jnp.logical_not(in_first))
        def _():
            o_refs[1][...] = out


def _ffn(xs, gains2, wg, wu, wd, layer, split_out=None):
    xs = tuple(xs) if isinstance(xs, (tuple, list)) else (xs,)
    n = sum(x.shape[0] for x in xs)
    out_rows = (n,) if split_out is None else tuple(split_out)
    first_rows = xs[0].shape[0] if len(xs) == 2 else out_rows[0]
    tm = _token_tile(first_rows, largest=FFN_TILE)
    assert all(r % tm == 0 for r in out_rows) and all(x.shape[0] % tm == 0 for x in xs) and sum(out_rows) == n
    first_tiles = first_rows // tm

    def specs(count):
        if count == 1:
            return [pl.BlockSpec((tm, D_MODEL), lambda i: (i, 0))]
        return [pl.BlockSpec((tm, D_MODEL), lambda i: (jnp.minimum(i, first_tiles - 1), 0)),
                pl.BlockSpec((tm, D_MODEL), lambda i: (jnp.maximum(i - first_tiles, 0), 0))]

    out = pl.pallas_call(
        functools.partial(_ffn_body, first_tiles, len(xs), len(out_rows)),
        out_shape=tuple(jax.ShapeDtypeStruct((r, D_MODEL), F32) for r in out_rows),
        grid=(n // tm,),
        in_specs=specs(len(xs)) + [
            _resident((2, D_MODEL)),
            _resident_layer(wg, layer),
            _resident_layer(wu, layer),
            _resident_layer(wd, layer),
        ],
        out_specs=tuple(specs(len(out_rows))),
        scratch_shapes=[pltpu.VMEM((tm, D_FF), BF16)],
        compiler_params=pltpu.CompilerParams(dimension_semantics=("arbitrary",), vmem_limit_bytes=VMEM_LIMIT),
        name="ffn",
    )(*xs, gains2, wg, wu, wd)
    return out[0] if split_out is None else out


W_PREP_ROWS = 1664


def _transpose_cast_body(wt_ref, o_ref):
    o_ref[...] = wt_ref[...].T.astype(BF16)


def _transpose_pad_body(wt_ref, o_ref):
    o_ref[...] = jnp.zeros(o_ref.shape, F32)
    o_ref[:, 0:M_HEADS] = wt_ref[...].T


def _prep_w_in(w_in):
    wt = jnp.swapaxes(w_in, 1, 2)
    w_dt = pl.pallas_call(
        _transpose_pad_body,
        out_shape=jax.ShapeDtypeStruct((DEPTH, D_MODEL, DT_PAD), F32),
        grid=(DEPTH,),
        in_specs=[pl.BlockSpec((None, M_HEADS, D_MODEL), lambda l: (l, D_PROJ_MAIN // M_HEADS, 0))],
        out_specs=pl.BlockSpec((None, D_MODEL, DT_PAD), lambda l: (l, 0, 0)),
        compiler_params=pltpu.CompilerParams(dimension_semantics=("arbitrary",)),
        name="w_dt_relayout",
    )(wt)
    w_main = pl.pallas_call(
        _transpose_cast_body,
        out_shape=jax.ShapeDtypeStruct((DEPTH, D_MODEL, D_PROJ_MAIN), BF16),
        grid=(DEPTH, D_PROJ_MAIN // W_PREP_ROWS),
        in_specs=[pl.BlockSpec((None, W_PREP_ROWS, D_MODEL), lambda l, j: (l, j, 0))],
        out_specs=pl.BlockSpec((None, D_MODEL, W_PREP_ROWS), lambda l, j: (l, 0, j)),
        compiler_params=pltpu.CompilerParams(dimension_semantics=("arbitrary", "arbitrary"),
                                             vmem_limit_bytes=VMEM_LIMIT),
        name="w_in_relayout",
    )(wt)
    return w_main, w_dt


_C_Q = 0
_C_F = HG_WIDTH
_C_I = 2 * HG_WIDTH
_C_G = 3 * HG_WIDTH
_C_Z = 4 * HG_WIDTH
_C_X = _C_Z + M_DINNER


def _inproj_body(layer, x_ref, g_ref, lbl_ref, dtb_ref, w_ref, wdt_ref,
                 q_ref, k_ref, gl_ref, v_ref, go_ref, z_ref, xbc_ref, dt_ref, xn_ref):
    xn_ref[...] = _rms(x_ref[...], g_ref[...]).astype(BF16)

    def proj(c0, width):
        return jnp.dot(xn_ref[...], w_ref[:, c0:c0 + width], preferred_element_type=F32)

    lg = lbl_ref[...]
    e = jnp.exp(lg - jnp.max(lg, axis=0, keepdims=True))
    p = e / jnp.sum(e, axis=0, keepdims=True)
    lb = jnp.sum(p[0:layer + 1, :], axis=0, keepdims=True) - p[0:1, :]
    lb_floor = jnp.maximum(lb, LB_FLOOR)

    def head_store(ref, c, val):
        for j in range(INPROJ_COLS // HG_KDIM):
            ref[c // HG_KDIM + j] = val[:, j * HG_KDIM:(j + 1) * HG_KDIM].astype(ref.dtype)

    def gate_block(c):
        cols = slice(c, c + INPROJ_COLS)
        fr = proj(_C_F + c, INPROJ_COLS)
        t = jnp.exp(-jnp.abs(fr))
        big = 1.0 / (1.0 + t)
        small = t * big
        positive = fr >= 0.0
        kk = (1.0 - lb[:, cols]) * jnp.where(positive, small, big)
        f_direct = lb_floor[:, cols] + (1.0 - lb[:, cols]) * jnp.where(positive, big, small)
        glog = jnp.where(positive, jnp.log1p((lb_floor[:, cols] - lb[:, cols]) - kk), jnp.log(f_direct))
        head_store(k_ref, c, kk)
        head_store(gl_ref, c, glog * LOG2E)

    def xbc_block(c):
        xbc_ref[:, c:c + INPROJ_COLS] = proj(_C_X + c, INPROJ_COLS)

    xbc_blocks = list(range(0, CONV_DIM, INPROJ_COLS))
    for c in range(0, HG_WIDTH, INPROJ_COLS):
        head_store(q_ref, c, proj(_C_Q + c, INPROJ_COLS) * (HG_KDIM ** -0.5))
        gate_block(c)
        xbc_block(xbc_blocks.pop())
        head_store(v_ref, c, proj(_C_I + c, INPROJ_COLS))
        head_store(go_ref, c, _silu(proj(_C_G + c, INPROJ_COLS)))
        z_ref[:, c:c + INPROJ_COLS] = _silu(proj(_C_Z + c, INPROJ_COLS)).astype(BF16)
    for c in xbc_blocks:
        xbc_block(c)
    dt_ref[...] = jax.nn.softplus(
        jnp.dot(xn_ref[...], wdt_ref[...].astype(BF16), preferred_element_type=F32) + dtb_ref[...])


def _inproj(h, gain, lb_logits, dt_bias_pad, w, w_dt, layer):
    n = h.shape[0]
    tm = _token_tile(n, largest=INPROJ_TILE)

    def heads(dtype):
        return jax.ShapeDtypeStruct((HG_HEADS, n, HG_KDIM), dtype)

    head_spec = pl.BlockSpec((HG_HEADS, tm, HG_KDIM), lambda i: (0, i, 0))

    def tok(width):
        return pl.BlockSpec((tm, width), lambda i: (i, 0))

    return pl.pallas_call(
        functools.partial(_inproj_body, layer),
        out_shape=(heads(F32), heads(F32), heads(F32), heads(BF16), heads(BF16),
                   jax.ShapeDtypeStruct((n, M_DINNER), BF16),
                   jax.ShapeDtypeStruct((n, CONV_DIM), F32),
                   jax.ShapeDtypeStruct((n, DT_PAD), F32)),
        grid=(n // tm,),
        in_specs=[
            tok(D_MODEL),
            _resident((1, D_MODEL)),
            _resident((DEPTH, HG_WIDTH)),
            _resident((1, DT_PAD)),
            _resident_layer(w, layer),
            _resident_layer(w_dt, layer),
        ],
        out_specs=(head_spec,) * 5 + (tok(M_DINNER), tok(CONV_DIM), tok(DT_PAD)),
        scratch_shapes=[pltpu.VMEM((tm, D_MODEL), BF16)],
        compiler_params=pltpu.CompilerParams(dimension_semantics=("arbitrary",), vmem_limit_bytes=VMEM_LIMIT),
        name="inproj",
    )(h, gain, lb_logits, dt_bias_pad, w, w_dt)


BAND_LONG = 4
BAND_SHORT = 2


def _expand_heads(vals, group):
    rows = vals.shape[0]
    lane = lax.broadcasted_iota(jnp.int32, (rows, LANES), 1)
    tiles = []
    for j in range(M_GROUP_WIDTH // LANES):
        h0 = group * (M_HEADS // M_GROUPS) + 2 * j
        lo = jnp.broadcast_to(vals[:, h0:h0 + 1], (rows, LANES))
        hi = jnp.broadcast_to(vals[:, h0 + 1:h0 + 2], (rows, LANES))
        tiles.append(jnp.where(lane < M_HEADDIM, lo, hi))
    return jnp.concatenate(tiles, axis=1)


def _as_column(row_vec):
    n = row_vec.shape[1]
    hi = row_vec.astype(BF16).astype(F32)
    mid = (row_vec - hi).astype(BF16).astype(F32)
    lo = row_vec - hi - mid
    sub = lax.broadcasted_iota(jnp.int32, (SUBLANES, n), 0)
    pieces = jnp.where(sub == 0, hi, jnp.where(sub == 1, mid, jnp.where(sub == 2, lo, 0.0)))
    return _mm_tn(pieces, jnp.ones((SUBLANES, n), F32))


_MIXER_INPUTS = 17


def _mixer_body(nseq, seq_chunk, steps, *refs):
    band = BAND_LONG if nseq == 1 else BAND_SHORT
    (q_ref, k_ref, gl_ref, v_ref, go_ref, z_ref, xbc_ref, dt_ref, hg0_ref, ssm0_ref, conv0_ref,
     convw_ref, convb_ref, alog_ref, dskip_ref, hgw_ref, ssmw_ref) = refs[:_MIXER_INPUTS]
    (ohg_ref, yssm_ref, hg_out_ref, ssm_out_ref, conv_out_ref,
     st_hg, st_ssm, cat_ref, shift_ref, mask_ref, class_ref, act_ref) = refs[-12:]
    chunk = nseq * seq_chunk
    step = pl.program_id(1)
    last_step = steps - 1
    heads_per_group = M_HEADS // M_GROUPS
    carried = steps > 1

    def when_step(which):
        if carried:
            return pl.when(step == which)
        return lambda block: block()

    def group_heads(g):
        return slice(g * heads_per_group, (g + 1) * heads_per_group)

    def read_hg(i, h):
        return st_hg[i, h] if carried else hg0_ref[i, h]

    def write_hg(i, h, val):
        if carried:
            st_hg[i, h] = val
        else:
            hg_out_ref[i, h] = val

    def read_ssm(i, g):
        return st_ssm[i, g] if carried else ssm0_ref[i, group_heads(g)].reshape(M_GROUP_WIDTH, M_DSTATE)

    def write_ssm(i, g, val):
        if carried:
            st_ssm[i, g] = val
        else:
            ssm_out_ref[i, group_heads(g)] = val.reshape(heads_per_group, M_HEADDIM, M_DSTATE)

    @when_step(0)
    def _load_states():
        for i in range(nseq):
            if carried:
                for h in range(HG_HEADS):
                    st_hg[i, h] = hg0_ref[i, h]
                for g in range(M_GROUPS):
                    st_ssm[i, g] = ssm0_ref[i, group_heads(g)].reshape(M_GROUP_WIDTH, M_DSTATE)
            for c in range(CONV_DIM // LANES):
                cat_ref[i, c, 0:SUBLANES - CONV_TAIL, :] = jnp.zeros((SUBLANES - CONV_TAIL, LANES), F32)
                cat_ref[i, c, SUBLANES - CONV_TAIL:SUBLANES, :] = conv0_ref[i, :, c * LANES:(c + 1) * LANES]
        shift_ref[:, :, 0:SUBLANES, :] = jnp.zeros((HG_HEADS, 2, SUBLANES, HG_KDIM), F32)

    levels = []
    m = band
    while 2 * m <= seq_chunk:
        levels.append(m)
        m *= 2

    @when_step(0)
    def _build_masks():
        row = lax.broadcasted_iota(jnp.int32, (chunk, chunk), 0)
        col = lax.broadcasted_iota(jnp.int32, (chunk, chunk), 1)
        causal = (col <= row) & ((row // seq_chunk) == (col // seq_chunk))
        mask_ref[...] = causal.astype(F32).astype(BF16)
        pair_class = jnp.full((chunk, chunk), -1, jnp.int32)
        for lvl in reversed(range(len(levels))):
            blk_log2 = jnp.int32((2 * levels[lvl]).bit_length() - 1)
            same_blk = lax.shift_right_logical(row, blk_log2) == lax.shift_right_logical(col, blk_log2)
            pair_class = jnp.where(same_blk, band + lvl, pair_class)
        pair_class = jnp.where(row - col < band, row - col, pair_class)
        class_ref[...] = jnp.where(causal, pair_class, -1)

    seq_rows = [slice(i * seq_chunk, (i + 1) * seq_chunk) for i in range(nseq)]

    def cumsum_rows(x):
        hi = x.astype(BF16)
        rest = x - hi.astype(F32)
        mid = rest.astype(BF16)
        lo = (rest - mid.astype(F32)).astype(BF16)
        return sum(jnp.dot(mask_ref[...], piece, preferred_element_type=F32) for piece in (hi, mid, lo))

    for h in range(HG_HEADS):
        shift_ref[h, 0, SUBLANES:SUBLANES + chunk, :] = cumsum_rows(gl_ref[h])
        shift_ref[h, 1, SUBLANES:SUBLANES + chunk, :] = k_ref[h]

    def gla_head(h):
        q = q_ref[h]
        k = k_ref[h]
        v = v_ref[h]
        b = shift_ref[h, 0, SUBLANES:SUBLANES + chunk, :]
        qe = q * jnp.exp2(b)
        o = jnp.concatenate([_mm(qe[r], read_hg(i, h)) for i, r in enumerate(seq_rows)], axis=0)

        scores = jnp.zeros((chunk, chunk), F32)
        for dist in range(band):
            if dist == 0:
                p = jnp.sum(q * k, axis=-1, keepdims=True)
            else:
                lo = SUBLANES - dist
                d = b - shift_ref[h, 0, lo:lo + chunk, :]
                if nseq > 1:
                    d = jnp.minimum(d, 0.0)
                p = jnp.sum(q * shift_ref[h, 1, lo:lo + chunk, :] * jnp.exp2(d), axis=-1, keepdims=True)
            scores = jnp.where(class_ref[...] == dist, p, scores)

        for lvl, m in enumerate(levels):
            blk = 2 * m
            bref = jnp.concatenate(
                [jnp.broadcast_to(b[j * blk + m - 1:j * blk + m, :], (blk, HG_KDIM)) for j in range(chunk // blk)],
                axis=0)
            e = jnp.exp2(-jnp.abs(b - bref))
            scores = jnp.where(class_ref[...] == band + lvl, _mm_nt(q * e, k * e), scores)
        o = o + _mm(scores, v)

        for i, r in enumerate(seq_rows):
            b_last = b[r.stop - 1:r.stop, :]
            kd = k[r] * jnp.exp2(b_last - b[r])
            write_hg(i, h, _as_column(jnp.exp2(b_last)) * read_hg(i, h) + _mm_tn(kd, v[r]))

        on = _rms(o, hgw_ref[h]) * go_ref[h]
        ohg_ref[:, h * HG_VDIM:(h + 1) * HG_VDIM] = on.astype(BF16)

    for i, r in enumerate(seq_rows):
        for c in range(CONV_DIM // LANES):
            lanes = slice(c * LANES, (c + 1) * LANES)
            cat_ref[i, c, SUBLANES:SUBLANES + seq_chunk, :] = xbc_ref[r, lanes]
            conv = convb_ref[:, lanes]
            for j in range(CONV_W):
                off = SUBLANES - (CONV_W - 1) + j
                conv = conv + cat_ref[i, c, off:off + seq_chunk, :] * convw_ref[j:j + 1, lanes]
            act_ref[r, lanes] = _silu(conv)
            cat_ref[i, c, 0:SUBLANES, :] = cat_ref[i, c, seq_chunk:seq_chunk + SUBLANES, :]

    lane_t = lax.broadcasted_iota(jnp.int32, (chunk, DT_PAD), 1)
    dt = jnp.where(lane_t < M_HEADS, dt_ref[...], 0.0)
    da = dt * (-LOG2E * jnp.exp(alog_ref[...]))
    cum = cumsum_rows(da)
    key_term = cum.T[0:M_HEADS, :] - jnp.log2(dt.T[0:M_HEADS, :])
    ecum = jnp.exp2(cum)
    lane_x = lax.broadcasted_iota(jnp.int32, (chunk, LANES), 1)
    pairs_per_group = M_GROUP_WIDTH // LANES
    assert HG_HEADS == M_GROUPS * pairs_per_group

    def group_operands(g):
        b_g = act_ref[:, M_DINNER + g * M_DSTATE:M_DINNER + (g + 1) * M_DSTATE]
        c_g = act_ref[:, M_DINNER + (M_GROUPS + g) * M_DSTATE:M_DINNER + (M_GROUPS + g + 1) * M_DSTATE]
        return b_g, c_g, act_ref[:, g * M_GROUP_WIDTH:(g + 1) * M_GROUP_WIDTH]

    cb_groups = [_mm_nt(group_operands(g)[1], group_operands(g)[0]) for g in range(M_GROUPS)]

    def ssd_pair(g, j):
        x_pair = act_ref[:, g * M_GROUP_WIDTH + j * LANES:g * M_GROUP_WIDTH + (j + 1) * LANES]
        causal = class_ref[...] >= 0
        y_pair = None
        for half in range(2):
            hd = g * heads_per_group + 2 * j + half
            diff = (jnp.broadcast_to(cum[:, hd:hd + 1], (chunk, chunk))
                    - jnp.broadcast_to(key_term[hd:hd + 1, :], (chunk, chunk)))
            mh = cb_groups[g] * jnp.exp2(jnp.where(causal, diff, NEG_BIG))
            keep = (lane_x < M_HEADDIM) if half == 0 else (lane_x >= M_HEADDIM)
            part = _mm(mh, jnp.where(keep, x_pair, 0.0))
            y_pair = part if y_pair is None else y_pair + part
        return y_pair

    y_tiles = [[None] * pairs_per_group for _ in range(M_GROUPS)]
    for h in range(HG_HEADS):
        gla_head(h)
        g, j = divmod(h, pairs_per_group)
        y_tiles[g][j] = ssd_pair(g, j)

    y_groups = []
    for g in range(M_GROUPS):
        b_g, c_g, x_g = group_operands(g)
        y_intra = jnp.concatenate(y_tiles[g], axis=1)
        y_inter = jnp.concatenate([_mm_nt(c_g[r], read_ssm(i, g)) for i, r in enumerate(seq_rows)], axis=0)
        y_g = y_intra + y_inter * _expand_heads(ecum, g)
        y_g = y_g + dskip_ref[:, g * M_GROUP_WIDTH:(g + 1) * M_GROUP_WIDTH] * x_g
        y_g = y_g * z_ref[:, g * M_GROUP_WIDTH:(g + 1) * M_GROUP_WIDTH]
        y_groups.append(_rms(y_g, ssmw_ref[:, g * M_GROUP_WIDTH:(g + 1) * M_GROUP_WIDTH]))

        for i, r in enumerate(seq_rows):
            cum_last = cum[r.stop - 1:r.stop, :]
            wgt = dt[r] * jnp.exp2(cum_last - cum[r])
            xw = x_g[r] * _expand_heads(wgt, g)
            decay = jnp.exp2(cum_last)
            decay_rows = jnp.concatenate(
                [jnp.broadcast_to(decay[:, hd:hd + 1], (M_HEADDIM, M_DSTATE))
                 for hd in range(g * heads_per_group, (g + 1) * heads_per_group)], axis=0)
            write_ssm(i, g, decay_rows * read_ssm(i, g) + _mm_tn(xw, b_g[r]))
    yssm_ref[...] = jnp.concatenate(y_groups, axis=1).astype(BF16)

    @when_step(last_step)
    def _store_states():
        for i in range(nseq):
            if carried:
                for h in range(HG_HEADS):
                    hg_out_ref[i, h] = st_hg[i, h]
                for g in range(M_GROUPS):
                    ssm_out_ref[i, group_heads(g)] = st_ssm[i, g].reshape(heads_per_group, M_HEADDIM, M_DSTATE)
            for c in range(CONV_DIM // LANES):
                conv_out_ref[i, :, c * LANES:(c + 1) * LANES] = cat_ref[i, c, SUBLANES - CONV_TAIL:SUBLANES, :]


def _mixer(heads5, zs, xbc, dt, hg0, ssm0, conv0, convw, convb, alog, dskip, hgw, ssmw, prev_out, tok_out, *,
           layer, batch, seq_len, tok_offset):
    if seq_len >= LANES:
        nseq, seq_chunk = 1, LANES
    else:
        nseq, seq_chunk = SAMPLE_SEQS_PER_STEP, seq_len
    assert seq_len % seq_chunk == 0 and batch % nseq == 0 and seq_chunk % max(BAND_LONG, BAND_SHORT) == 0
    chunk = nseq * seq_chunk
    steps = seq_len // seq_chunk
    n = zs.shape[0]
    assert tok_offset % chunk == 0 and tok_offset + batch * seq_len <= n

    def tok_idx(b, s):
        return tok_offset // chunk + b * steps + s

    head_spec = pl.BlockSpec((HG_HEADS, chunk, HG_KDIM), lambda b, s: (0, tok_idx(b, s), 0))

    def tok(width):
        return pl.BlockSpec((chunk, width), lambda b, s: (tok_idx(b, s), 0))

    hg_spec = pl.BlockSpec((None, nseq, HG_HEADS, HG_KDIM, HG_VDIM), lambda b, s: (layer, b, 0, 0, 0))
    ssm_spec = pl.BlockSpec((None, nseq, M_HEADS, M_HEADDIM, M_DSTATE), lambda b, s: (layer, b, 0, 0, 0))
    conv_spec = pl.BlockSpec((None, nseq, CONV_TAIL, CONV_DIM), lambda b, s: (layer, b, 0, 0))
    prev_out = () if prev_out is None else tuple(prev_out)
    tok_out = () if tok_out is None else tuple(tok_out)
    first_state_out = 2
    assert len(prev_out) in (0, 3) and len(tok_out) in (0, first_state_out)
    aliases = {_MIXER_INPUTS + j: first_state_out + j for j in range(len(prev_out))}
    aliases.update({_MIXER_INPUTS + len(prev_out) + j: j for j in range(len(tok_out))})

    return pl.pallas_call(
        functools.partial(_mixer_body, nseq, seq_chunk, steps),
        out_shape=(
            jax.ShapeDtypeStruct((n, HG_WIDTH), BF16),
            jax.ShapeDtypeStruct((n, M_DINNER), BF16),
            jax.ShapeDtypeStruct(hg0.shape, F32),
            jax.ShapeDtypeStruct(ssm0.shape, F32),
            jax.ShapeDtypeStruct(conv0.shape, F32),
        ),
        grid=(batch // nseq, steps),
        in_specs=[head_spec] * 5 + [tok(M_DINNER), tok(CONV_DIM), tok(DT_PAD), hg_spec, ssm_spec, conv_spec,
                                    _resident((CONV_W, CONV_DIM)), _resident((1, CONV_DIM)), _resident((1, DT_PAD)),
                                    _resident((1, M_DINNER)), _resident((HG_HEADS, 1, HG_VDIM)), _resident((1, M_DINNER))]
        + [pl.BlockSpec(memory_space=pl.ANY)] * (len(prev_out) + len(tok_out)),
        out_specs=(tok(HG_WIDTH), tok(M_DINNER), hg_spec, ssm_spec, conv_spec),
        input_output_aliases=aliases,
        scratch_shapes=[
            pltpu.VMEM((nseq if steps > 1 else 0, HG_HEADS, HG_KDIM, HG_VDIM), F32),
            pltpu.VMEM((nseq if steps > 1 else 0, M_GROUPS, M_GROUP_WIDTH, M_DSTATE), F32),
            pltpu.VMEM((nseq, CONV_DIM // LANES, SUBLANES + seq_chunk, LANES), F32),
            pltpu.VMEM((HG_HEADS, 2, SUBLANES + chunk, HG_KDIM), F32),
            pltpu.VMEM((chunk, chunk), BF16),
            pltpu.VMEM((chunk, chunk), jnp.int32),
            pltpu.VMEM((chunk, CONV_DIM), F32),
        ],
        compiler_params=pltpu.CompilerParams(dimension_semantics=("arbitrary", "arbitrary"),
                                             vmem_limit_bytes=VMEM_LIMIT),
        name="mixer",
    )(*heads5, zs, xbc, dt, hg0, ssm0, conv0, convw, convb, alog, dskip, hgw, ssmw, *prev_out, *tok_out)


def _outproj_body(h_ref, ohg_ref, yssm_ref, g_ref, w_ref, o_ref):
    m = jnp.dot(ohg_ref[...], w_ref[0:HG_WIDTH, :], preferred_element_type=F32)
    m = m + jnp.dot(yssm_ref[...], w_ref[HG_WIDTH:HG_WIDTH + M_DINNER, :], preferred_element_type=F32)
    o_ref[...] = h_ref[...] + _rms(m, g_ref[...])


def _outproj(h, ohg, yssm, gain, w, layer):
    n = h.shape[0]
    tm = _token_tile(n, largest=OUTPROJ_TILE)

    def tok(width):
        return pl.BlockSpec((tm, width), lambda i: (i, 0))

    return pl.pallas_call(
        _outproj_body,
        out_shape=jax.ShapeDtypeStruct((n, D_MODEL), F32),
        grid=(n // tm,),
        in_specs=[tok(D_MODEL), tok(HG_WIDTH), tok(M_DINNER), _resident((1, D_MODEL)),
                  _resident_layer(w, layer)],
        out_specs=tok(D_MODEL),
        compiler_params=pltpu.CompilerParams(dimension_semantics=("arbitrary",), vmem_limit_bytes=VMEM_LIMIT),
        name="outproj",
    )(h, ohg, yssm, gain, w)


def _prep_layers(w_in, *rest):
    w_main, w_dt = _prep_w_in(w_in)
    return [_prep_layer(l, w_main, w_dt, *rest) for l in range(DEPTH)]


def _prep_layer(l, w_main, w_dt, hg_lb_logits, conv_w, conv_b, dt_bias, a_log, d_skip, hg_norm_w, ssm_norm_w, w_out,
                f1g, f1u, f1d, f2g, f2u, f2d, norm_gain):
    pad = DT_PAD - M_HEADS
    return dict(
        layer=l,
        w_in=w_main,
        w_dt=w_dt,
        lb_logits=hg_lb_logits,
        conv_w=conv_w[l],
        conv_b=conv_b[l][None, :],
        dt_bias=jnp.pad(dt_bias[l], (0, pad))[None, :],
        a_log=jnp.pad(a_log[l], (0, pad))[None, :],
        d_skip=jnp.repeat(d_skip[l], M_HEADDIM)[None, :],
        hg_norm_w=hg_norm_w[l].reshape(HG_HEADS, 1, HG_VDIM),
        ssm_norm_w=ssm_norm_w[l][None, :],
        w_out=w_out.astype(BF16),
        ffn1=(f1g.astype(BF16), f1u.astype(BF16), f1d.astype(BF16)),
        ffn2=(f2g.astype(BF16), f2u.astype(BF16), f2d.astype(BF16)),
        gains=norm_gain[l],
    )


def _trunks(xs, states0, layers):
    shapes = [x.shape[:2] for x in xs]
    rows = [b * l for b, l in shapes]
    offsets = [sum(rows[:g]) for g in range(len(xs))]
    h = tuple(x.reshape(r, D_MODEL) for x, r in zip(xs, rows))
    states = [None] * len(xs)
    for p in layers:
        gains, l = p["gains"], p["layer"]
        h = _ffn(h, gains[0:2], *p["ffn1"], l)
        *heads5, zs, xbc, dt = _inproj(h, gains[2:3], p["lb_logits"], p["dt_bias"], p["w_in"], p["w_dt"], l)
        tok_out = None
        for g, (batch, seq_len) in enumerate(shapes):
            ohg, yssm, *states[g] = _mixer(
                heads5, zs, xbc, dt, *states0[g], p["conv_w"], p["conv_b"], p["a_log"], p["d_skip"],
                p["hg_norm_w"], p["ssm_norm_w"], states[g], tok_out,
                layer=l, batch=batch, seq_len=seq_len, tok_offset=offsets[g])
            tok_out = (ohg, yssm)
        h = _outproj(h, ohg, yssm, gains[3:4], p["w_out"], l)
        h = _ffn(h, gains[4:6], *p["ffn2"], l, split_out=rows if l == DEPTH - 1 else None)
    ys = tuple(y.reshape(b, s, D_MODEL) for y, (b, s) in zip(h, shapes))
    return ys, states


def kernel(x_prompt, x_sample, state_hgrn, state_ssm, state_conv, w_in, hg_lb_logits, conv_w, conv_b, dt_bias, a_log, d_skip, hg_norm_w, ssm_norm_w, w_out, ffn1_w_gate, ffn1_w_up, ffn1_w_down, ffn2_w_gate, ffn2_w_up, ffn2_w_down, norm_gain):
    layers = _prep_layers(w_in, hg_lb_logits, conv_w, conv_b, dt_bias, a_log, d_skip, hg_norm_w, ssm_norm_w, w_out,
                          ffn1_w_gate, ffn1_w_up, ffn1_w_down, ffn2_w_gate, ffn2_w_up, ffn2_w_down, norm_gain)
    bp = x_prompt.shape[0]
    zero_states = (jnp.zeros((DEPTH, bp, HG_HEADS, HG_KDIM, HG_VDIM), F32),
                   jnp.zeros((DEPTH, bp, M_HEADS, M_HEADDIM, M_DSTATE), F32),
                   jnp.zeros((DEPTH, bp, CONV_W - 1, CONV_DIM), F32))
    (y_prompt, y_sample), (states_p, states_s) = _trunks(
        (x_prompt, x_sample), (zero_states, (state_hgrn, state_ssm, state_conv)), layers)
    return (y_prompt, y_sample, *states_p, *states_s)
```
